```python
import jax, jax.numpy as jnp
from jax import lax
import numpy as np

D_MODEL = 1024
BATCH = 8
SEQ = 2048
DEPTH = 1
DEC_BATCH = 128
DEC_SEQ = 4
PAST_LEN = 16384
PAGE_SIZE = 128

D_CONV = D_MODEL
CONV_W = 31
HEAD_N = 64
N_HEADS = D_MODEL // HEAD_N
D_R = N_HEADS * HEAD_N
R_DECAY = 64
R_ICLR = 64
R_GATE = 128
N_SHIFT = 3 * D_R + R_DECAY + R_ICLR + R_GATE
N_IN = 2 * D_CONV + N_SHIFT + 2 * D_MODEL
D_FF = 4 * D_MODEL
RMS_EPS = 1e-6
LN_EPS = 1e-5
GN_EPS = 64e-5

kernel_name = "conformer_rwkv7_gated_hybrid_step"


def _rms_norm(x, g):
    xf = x.astype(jnp.float32)
    y = xf * lax.rsqrt(jnp.mean(xf * xf, axis=-1, keepdims=True) + RMS_EPS)
    return (y * g).astype(x.dtype)


def _layer_norm(x, g, b):
    xf = x.astype(jnp.float32)
    mu = jnp.mean(xf, axis=-1, keepdims=True)
    xc = xf - mu
    var = jnp.mean(xc * xc, axis=-1, keepdims=True)
    return (xc * lax.rsqrt(var + LN_EPS) * g + b).astype(x.dtype)


def _conformer_branch(u_val, u_gate, conv_buf, conv_w, conv_b, ln_g, ln_b, w_conv_out):
    u = u_val * jax.nn.sigmoid(u_gate)
    ext = jnp.concatenate([conv_buf.astype(u.dtype), u], axis=1)
    c = lax.conv_general_dilated(
        ext, conv_w[:, None, :].astype(u.dtype), window_strides=(1,), padding='VALID',
        dimension_numbers=('NWC', 'WIO', 'NWC'), feature_group_count=D_CONV) + conv_b
    c = jax.nn.silu(_layer_norm(c, ln_g, ln_b))
    return c @ w_conv_out, ext[:, -(CONV_W - 1):]


def _wkv_scan(r, decay, k, v, kk, a, s0):
    def step(s, inp):
        r_t, w_t, k_t, v_t, kk_t, a_t = inp
        sk = jnp.einsum('bhvk,bhk->bhv', s, kk_t)
        s = (s * w_t[:, :, None, :]
             - sk[..., None] * (kk_t * a_t)[:, :, None, :]
             + v_t[..., None] * k_t[:, :, None, :])
        return s, jnp.einsum('bhvk,bhk->bhv', s, r_t)
    xs = tuple(jnp.swapaxes(t, 0, 1) for t in (r, decay, k, v, kk, a))
    s, y = lax.scan(step, s0, xs)
    return jnp.swapaxes(y, 0, 1), s


def _mixer(h, conv_buf, shift_row, wkv, w_in, conv_w, conv_b, conv_ln_g, conv_ln_b, w_conv_out,
           shift_mu, decay_base, w_decay_up, iclr_base, w_iclr_up, w_gate_up, k_k, k_a, r_k,
           lnx_g, lnx_b, w_rwkv_out, w_out):
    B, T, _ = h.shape
    i0 = 2 * D_CONV
    i1 = i0 + N_SHIFT
    proj = h @ w_in
    o_a, new_buf = _conformer_branch(proj[..., :D_CONV], proj[..., D_CONV:i0], conv_buf,
                                     conv_w, conv_b, conv_ln_g, conv_ln_b, w_conv_out)
    p_rw = proj[..., i0:i1]
    p_prev0 = shift_row.astype(h.dtype) @ w_in[:, i0:i1]
    p_prev = jnp.concatenate([p_prev0[:, None], p_rw[:, :-1]], axis=1)
    p_mix = (p_rw + (p_prev - p_rw) * shift_mu).astype(jnp.float32)
    r, k, v, l_dec, l_iclr, l_gate = jnp.split(
        p_mix, [D_R, 2 * D_R, 3 * D_R, 3 * D_R + R_DECAY, 3 * D_R + R_DECAY + R_ICLR], axis=-1)
    w_log = -jax.nn.softplus(-(decay_base + jnp.tanh(l_dec) @ w_decay_up)) - 0.5
    decay = jnp.exp(-jnp.exp(w_log))
    a = jax.nn.sigmoid(iclr_base + l_iclr @ w_iclr_up)
    g = jax.nn.sigmoid(l_gate) @ w_gate_up
    hs = lambda t: t.reshape(B, T, N_HEADS, HEAD_N)
    r, k, v, a, decay = hs(r), hs(k), hs(v), hs(a), hs(decay)
    kk = k * k_k
    kk = kk * lax.rsqrt(jnp.maximum(jnp.sum(kk * kk, axis=-1, keepdims=True), 1e-24))
    k = k * (1.0 + (a - 1.0) * k_a)
    y, s_new = _wkv_scan(r, decay, k, v, kk, a, wkv.astype(jnp.float32))
    mu = jnp.mean(y, axis=-1, keepdims=True)
    yc = y - mu
    y_n = yc * lax.rsqrt(jnp.mean(yc * yc, axis=-1, keepdims=True) + GN_EPS)
    y_n = y_n.reshape(B, T, D_R) * lnx_g + lnx_b
    bonus = (jnp.sum(r * k * r_k, axis=-1, keepdims=True) * v).reshape(B, T, D_R)
    o_b = ((y_n + bonus) * g).astype(h.dtype) @ w_rwkv_out
    gate_a = jax.nn.sigmoid(proj[..., i1:i1 + D_MODEL])
    gate_b = jax.nn.sigmoid(proj[..., i1 + D_MODEL:])
    merged = gate_a * o_a + gate_b * o_b
    return merged @ w_out, new_buf, h[:, -1], s_new.astype(wkv.dtype)


def _block(x, conv_buf, shift_row, wkv, pre_mix_g, post_mix_g, pre_ffn_g, post_ffn_g, w_in,
           conv_w, conv_b, conv_ln_g, conv_ln_b, w_conv_out, shift_mu, decay_base, w_decay_up,
           iclr_base, w_iclr_up, w_gate_up, k_k, k_a, r_k, lnx_g, lnx_b, w_rwkv_out, w_out,
           w_ff_up, w_ff_down):
    h = _rms_norm(x, pre_mix_g)
    m, new_buf, new_shift, new_wkv = _mixer(
        h, conv_buf, shift_row, wkv, w_in, conv_w, conv_b, conv_ln_g, conv_ln_b, w_conv_out,
        shift_mu, decay_base, w_decay_up, iclr_base, w_iclr_up, w_gate_up, k_k, k_a, r_k,
        lnx_g, lnx_b, w_rwkv_out, w_out)
    x = x + _rms_norm(m, post_mix_g)
    h2 = _rms_norm(x, pre_ffn_g)
    f = jnp.square(jax.nn.relu(h2 @ w_ff_up)) @ w_ff_down
    x = x + _rms_norm(f, post_ffn_g)
    return x, new_buf, new_shift, new_wkv


def setup_inputs(seed: int = 0) -> dict:
    key = jax.random.key(seed)
    ks = jax.random.split(key, 40)
    f32 = jnp.float32
    nrm = lambda i, shape, s: (jax.random.normal(ks[i], shape, f32) * s)
    gain = lambda i, shape: 1.0 + nrm(i, shape, 0.05)
    L = DEPTH
    return {
        "x_prompt": nrm(0, (BATCH, SEQ, D_MODEL), 1.0),
        "x_sample": nrm(1, (DEC_BATCH, DEC_SEQ, D_MODEL), 1.0),
        "state_conv": nrm(2, (L, DEC_BATCH, CONV_W - 1, D_CONV), 0.5),
        "state_shift": nrm(3, (L, DEC_BATCH, D_MODEL), 1.0),
        "state_wkv": nrm(4, (L, DEC_BATCH, N_HEADS, HEAD_N, HEAD_N), 1.0),
        "pre_mix_g": gain(5, (L, D_MODEL)),
        "post_mix_g": gain(6, (L, D_MODEL)),
        "pre_ffn_g": gain(7, (L, D_MODEL)),
        "post_ffn_g": gain(8, (L, D_MODEL)),
        "w_in": nrm(9, (L, D_MODEL, N_IN), D_MODEL ** -0.5),
        "conv_w": nrm(10, (L, CONV_W, D_CONV), CONV_W ** -0.5),
        "conv_b": nrm(11, (L, D_CONV), 0.02),
        "conv_ln_g": gain(12, (L, D_CONV)),
        "conv_ln_b": nrm(13, (L, D_CONV), 0.02),
        "w_conv_out": nrm(14, (L, D_CONV, D_MODEL), D_CONV ** -0.5),
        "shift_mu": jax.random.uniform(ks[15], (L, N_SHIFT), f32),
        "decay_base": -1.0 + nrm(16, (L, D_R), 0.5),
        "w_decay_up": nrm(17, (L, R_DECAY, D_R), 0.1),
        "iclr_base": nrm(18, (L, D_R), 0.1),
        "w_iclr_up": nrm(19, (L, R_ICLR, D_R), 0.1),
        "w_gate_up": nrm(20, (L, R_GATE, D_R), R_GATE ** -0.5),
        "k_k": 0.85 + nrm(21, (L, N_HEADS, HEAD_N), 0.05),
        "k_a": gain(22, (L, N_HEADS, HEAD_N)),
        "r_k": nrm(23, (L, N_HEADS, HEAD_N), 0.1),
        "lnx_g": gain(24, (L, D_R)),
        "lnx_b": nrm(25, (L, D_R), 0.02),
        "w_rwkv_out": nrm(26, (L, D_R, D_MODEL), D_R ** -0.5),
        "w_out": nrm(27, (L, D_MODEL, D_MODEL), D_MODEL ** -0.5),
        "w_ff_up": nrm(28, (L, D_MODEL, D_FF), D_MODEL ** -0.5),
        "w_ff_down": nrm(29, (L, D_FF, D_MODEL), D_FF ** -0.5),
    }


def reference(x_prompt, x_sample, state_conv, state_shift, state_wkv, pre_mix_g, post_mix_g,
              pre_ffn_g, post_ffn_g, w_in, conv_w, conv_b, conv_ln_g, conv_ln_b, w_conv_out,
              shift_mu, decay_base, w_decay_up, iclr_base, w_iclr_up, w_gate_up, k_k, k_a, r_k,
              lnx_g, lnx_b, w_rwkv_out, w_out, w_ff_up, w_ff_down):
    xp = x_prompt
    xs = x_sample
    conv_p, shift_p, wkv_p = [], [], []
    conv_s, shift_s, wkv_s = [], [], []
    for l in range(DEPTH):
        lw = (pre_mix_g[l], post_mix_g[l], pre_ffn_g[l], post_ffn_g[l], w_in[l], conv_w[l],
              conv_b[l], conv_ln_g[l], conv_ln_b[l], w_conv_out[l], shift_mu[l], decay_base[l],
              w_decay_up[l], iclr_base[l], w_iclr_up[l], w_gate_up[l], k_k[l], k_a[l], r_k[l],
              lnx_g[l], lnx_b[l], w_rwkv_out[l], w_out[l], w_ff_up[l], w_ff_down[l])
        b = xp.shape[0]
        z_conv = jnp.zeros((b, CONV_W - 1, D_CONV), xp.dtype)
        z_shift = jnp.zeros((b, D_MODEL), xp.dtype)
        z_wkv = jnp.zeros((b, N_HEADS, HEAD_N, HEAD_N), xp.dtype)
        xp, cb, sr, sw = _block(xp, z_conv, z_shift, z_wkv, *lw)
        conv_p.append(cb); shift_p.append(sr); wkv_p.append(sw)
        xs, cb, sr, sw = _block(xs, state_conv[l], state_shift[l], state_wkv[l], *lw)
        conv_s.append(cb); shift_s.append(sr); wkv_s.append(sw)
    return (xp, xs, jnp.stack(conv_p), jnp.stack(shift_p), jnp.stack(wkv_p),
            jnp.stack(conv_s), jnp.stack(shift_s), jnp.stack(wkv_s))
```

```python
import functools
import math

import jax
import jax.numpy as jnp
from jax import lax
from jax.experimental import pallas as pl
from jax.experimental.pallas import tpu as pltpu

F32 = jnp.float32
BF16 = jnp.bfloat16

RMS_EPS = 1e-6
LN_EPS = 1e-5
GN_EPS = 64e-5
HEAD_N = 64
LANES = 128
CHUNK = 64
CONV_HALO = 32
VMEM_LIMIT = 52 * 1024 * 1024


def _cparams(n_axes):
    return pltpu.CompilerParams(dimension_semantics=("arbitrary",) * n_axes,
                                vmem_limit_bytes=VMEM_LIMIT)


def _rms(x, g):
    return x * lax.rsqrt(jnp.mean(x * x, axis=-1, keepdims=True) + RMS_EPS) * g


def _sigmoid(x):
    return 1.0 / (1.0 + jnp.exp(-x))


def _dot(a, b):
    return jnp.dot(a, b, preferred_element_type=F32)


def _dot_nt(a, b):
    return lax.dot_general(a, b, (((1,), (1,)), ((), ())), preferred_element_type=F32)


def _dot_tn(a, b):
    return lax.dot_general(a, b, (((0,), (0,)), ((), ())), preferred_element_type=F32)


def _proj_kernel(x_ref, g_ref, w_ref, o_ref, *, apply_norm):
    x = x_ref[...]
    if apply_norm:
        x = _rms(x, g_ref[...])
    o_ref[...] = _dot(x.astype(BF16), w_ref[...])


def _norm_matmul(x, g, w, *, tm, tn, apply_norm, name):
    m, d = x.shape
    n = w.shape[1]
    assert m % tm == 0 and n % tn == 0
    return pl.pallas_call(
        functools.partial(_proj_kernel, apply_norm=apply_norm),
        grid=(n // tn, m // tm),
        in_specs=[pl.BlockSpec((tm, d), lambda j, i: (i, 0)),
                  pl.BlockSpec((1, d), lambda j, i: (0, 0)),
                  pl.BlockSpec((d, tn), lambda j, i: (0, j))],
        out_specs=pl.BlockSpec((tm, tn), lambda j, i: (i, j)),
        out_shape=jax.ShapeDtypeStruct((m, n), F32),
        compiler_params=_cparams(2),
        name=name,
    )(x, g, w)


def _rms_rows_kernel(x_ref, g_ref, o_ref):
    o_ref[...] = _rms(x_ref[...], g_ref[...])


def _rms_rows(x, g):
    return pl.pallas_call(_rms_rows_kernel, out_shape=jax.ShapeDtypeStruct(x.shape, F32),
                          name="shift_rows")(x, g)


def _conv_kernel(val_ref, gate_ref, st_ref, cw_ref, cb_ref, lg_ref, lb_ref, wo_ref,
                 oa_ref, tail_ref, ext_ref, c_ref, *, tt, t_valid, rb, n_taps):
    halo = n_taps - 1
    lead = CONV_HALO - halo

    @pl.when(pl.program_id(1) == 0)
    def _():
        ext_ref[0:lead, :] = jnp.zeros((lead, ext_ref.shape[1]), F32)
        ext_ref[lead:CONV_HALO, :] = st_ref[0]

    u = val_ref[0] * _sigmoid(gate_ref[0])
    ext_ref[CONV_HALO:CONV_HALO + tt, :] = u
    for r0 in range(0, tt, rb):
        acc = jnp.broadcast_to(cb_ref[...], (rb, cb_ref.shape[1]))
        for j in range(n_taps):
            acc = acc + ext_ref[lead + j + r0:lead + j + r0 + rb, :] * cw_ref[j:j + 1, :]
        c_ref[r0:r0 + rb, :] = acc
    c = c_ref[...]
    mu = jnp.mean(c, axis=-1, keepdims=True)
    cc = c - mu
    var = jnp.mean(cc * cc, axis=-1, keepdims=True)
    z = cc * lax.rsqrt(var + LN_EPS) * lg_ref[...] + lb_ref[...]
    z = z * _sigmoid(z)
    oa_ref[0] = _dot(z.astype(BF16), wo_ref[...])
    tail = ext_ref[lead + t_valid:lead + t_valid + halo, :]
    ext_ref[lead:CONV_HALO, :] = tail
    tail_ref[0] = tail


def _conv_branch(proj, conv_state, conv_w, conv_b, ln_g, ln_b, w_conv_out, *, tt, t_valid):
    b, t, _ = proj.shape
    d = conv_w.shape[1]
    n_taps = conv_w.shape[0]
    rb = min(tt, 32)
    kern = functools.partial(_conv_kernel, tt=tt, t_valid=t_valid, rb=rb, n_taps=n_taps)
    row = lambda shape: pl.BlockSpec(shape, lambda i, j: (0, 0))
    return pl.pallas_call(
        kern,
        grid=(b, t // tt),
        in_specs=[pl.BlockSpec((1, tt, d), lambda i, j: (i, j, 0)),
                  pl.BlockSpec((1, tt, d), lambda i, j: (i, j, 1)),
                  pl.BlockSpec((1, n_taps - 1, d), lambda i, j: (i, 0, 0)),
                  row((n_taps, d)), row((1, d)), row((1, d)), row((1, d)), row((d, d))],
        out_specs=[pl.BlockSpec((1, tt, d), lambda i, j: (i, j, 0)),
                   pl.BlockSpec((1, n_taps - 1, d), lambda i, j: (i, 0, 0))],
        out_shape=[jax.ShapeDtypeStruct((b, t, d), F32),
                   jax.ShapeDtypeStruct((b, n_taps - 1, d), F32)],
        scratch_shapes=[pltpu.VMEM((CONV_HALO + tt, d), F32), pltpu.VMEM((tt, d), F32)],
        compiler_params=_cparams(2),
        name="conv_branch",
    )(proj, proj, conv_state, conv_w, conv_b, ln_g, ln_b, w_conv_out)


def _split3(x):
    hi = x.astype(BF16)
    r1 = x - hi.astype(F32)
    mid = r1.astype(BF16)
    lo = (r1 - mid.astype(F32)).astype(BF16)
    return hi, mid, lo


def _rwkv_kernel(r_ref, k_ref, v_ref, lo_ref, prev_ref, s0_ref,
                 mu_ref, dbase_ref, wdec_ref, ibase_ref, wiclr_ref, wgate_ref,
                 kkw_ref, ka_ref, rk_ref, lng_ref, lnb_ref, wout_ref,
                 ob_ref, sout_ref,
                 sh_ref, r_s, kt_s, v_s, al_s, be_s, lw_s, g_s, bo_s, y_s, sbd_ref,
                 *, tt, t_valid, n_sq):
    d = r_ref.shape[2]
    n_lora = lo_ref.shape[2]
    n_pairs = d // LANES
    rows_s = r_s.shape[0]
    ti = pl.program_id(1)

    @pl.when(ti == 0)
    def _():
        sh_ref[7:8, :] = prev_ref[0]
        sbd_ref[...] = s0_ref[0]

    sh_ref[8:8 + tt, 0:d] = r_ref[0]
    sh_ref[8:8 + tt, d:2 * d] = k_ref[0]
    sh_ref[8:8 + tt, 2 * d:3 * d] = v_ref[0]
    sh_ref[8:8 + tt, 3 * d:3 * d + n_lora] = lo_ref[0]

    def mixed(c0, c1):
        cur = sh_ref[8:8 + tt, c0:c1]
        prv = sh_ref[7:7 + tt, c0:c1]
        return cur + (prv - cur) * mu_ref[:, c0:c1]

    r = mixed(0, d)
    k = mixed(d, 2 * d)
    v = mixed(2 * d, 3 * d)
    lora = mixed(3 * d, 3 * d + n_lora)
    sh_ref[7:8, :] = sh_ref[7 + t_valid:8 + t_valid, :]

    l_di = lora[:, 0:LANES]
    l_g = lora[:, LANES:2 * LANES]
    dec_in = dbase_ref[...] + _dot(jnp.tanh(l_di).astype(BF16), wdec_ref[...])
    a = _sigmoid(ibase_ref[...] + _dot(l_di.astype(BF16), wiclr_ref[...]))
    g = _dot(_sigmoid(l_g).astype(BF16), wgate_ref[...])
    z = -dec_in
    softplus = jnp.maximum(z, 0.0) + jnp.log(1.0 + jnp.exp(-jnp.abs(z)))
    lw = -jnp.exp(-softplus - 0.5)

    li = lax.broadcasted_iota(jnp.int32, (LANES, LANES), 0) // HEAD_N
    lj = lax.broadcasted_iota(jnp.int32, (LANES, LANES), 1) // HEAD_N
    same_head = li == lj
    ones_bd = jnp.where(same_head, 1.0, 0.0).astype(BF16)

    def head_sum(x):
        parts = [_dot(x[:, p * LANES:(p + 1) * LANES].astype(BF16), ones_bd) for p in range(n_pairs)]
        return jnp.concatenate(parts, axis=1)

    kk = k * kkw_ref[...]
    kk = kk * lax.rsqrt(jnp.maximum(head_sum(kk * kk), 1e-24))
    kt = k * (1.0 + (a - 1.0) * ka_ref[...])
    bonus = head_sum(r * kt * rk_ref[...]) * v
    alpha = -kk
    beta = kk * a
    if t_valid < tt:
        live = lax.broadcasted_iota(jnp.int32, (tt, d), 0) < t_valid
        zero = jnp.zeros((tt, d), F32)
        lw, kt, v, alpha, beta = (jnp.where(live, x, zero) for x in (lw, kt, v, alpha, beta))
    for ref, val in ((r_s, r), (kt_s, kt), (v_s, v), (al_s, alpha), (be_s, beta), (lw_s, lw)):
        ref[0:tt, :] = val
        if tt < rows_s:
            ref[tt:rows_s, :] = jnp.zeros((rows_s - tt, d), F32)
    g_s[...] = g
    bo_s[...] = bonus

    ci = lax.broadcasted_iota(jnp.int32, (CHUNK, CHUNK), 0)
    cj = lax.broadcasted_iota(jnp.int32, (CHUNK, CHUNK), 1)
    tril = jnp.where(cj <= ci, 1.0, 0.0).astype(BF16)
    pt = lax.broadcasted_iota(jnp.int32, (CHUNK, LANES), 0)
    pl_ = lax.broadcasted_iota(jnp.int32, (CHUNK, LANES), 1)
    ps = pl_ % HEAD_N
    strict = ps < pt
    incl = ps <= pt
    lane_lo = pl_ < HEAD_N
    eye_pair = jnp.where(ps == pt, 1.0, 0.0)
    zc = jnp.zeros((CHUNK, LANES), F32)

    def bd(x):
        return jnp.concatenate([jnp.where(lane_lo, x, zc).astype(BF16),
                                jnp.where(lane_lo, zc, x).astype(BF16)], axis=0)

    def chunk_body(c, carry):
        rows = pl.ds(pl.multiple_of(c * CHUNK, CHUNK), CHUNK)
        lwc = lw_s[rows, :]
        hi, mid, lo = _split3(lwc)
        cum = _dot(tril, hi) + _dot(tril, mid) + _dot(tril, lo)
        cum_last = cum[CHUNK - 1:CHUNK, :]
        dec_in_ = jnp.exp(cum)
        dec_inv = jnp.exp(-cum)
        dec_prev = jnp.exp(cum - lwc)
        dec_tail = jnp.exp(cum_last - cum)
        dec_all = jnp.exp(cum_last)
        al = al_s[rows, :]
        be = be_s[rows, :]
        ktc = kt_s[rows, :]
        a_t = al * dec_prev
        r_t = r_s[rows, :] * dec_in_
        b_h = be * dec_inv
        k_h = ktc * dec_inv
        b_e = (be * dec_tail).astype(BF16)
        k_e = (ktc * dec_tail).astype(BF16)
        vc = v_s[rows, :]
        for p in range(n_pairs):
            cs = slice(p * LANES, (p + 1) * LANES)
            lr = jnp.concatenate([a_t[:, cs].astype(BF16), r_t[:, cs].astype(BF16)], axis=0)
            rbk = jnp.concatenate([bd(b_h[:, cs]), bd(k_h[:, cs])], axis=0)
            ar = _dot_nt(lr, rbk)
            s_old = sbd_ref[p]
            x = _dot_nt(lr, s_old.astype(BF16))
            l_ab = jnp.where(strict, ar[0:CHUNK, 0:LANES], zc)
            l_ak = jnp.where(strict, ar[0:CHUNK, LANES:2 * LANES], zc)
            a_rb = jnp.where(incl, ar[CHUNK:2 * CHUNK, 0:LANES], zc)
            a_rk = jnp.where(incl, ar[CHUNK:2 * CHUNK, LANES:2 * LANES], zc)
            tm = eye_pair + l_ab
            if n_sq > 0:
                pw = _dot(l_ab.astype(BF16), bd(l_ab))
            for i in range(n_sq):
                if i == n_sq - 1:
                    tm = tm + _dot(pw.astype(BF16), bd(tm))
                else:
                    pr = _dot(pw.astype(BF16), jnp.concatenate([bd(pw), bd(tm)], axis=1))
                    pw = pr[:, 0:LANES]
                    tm = tm + pr[:, LANES:2 * LANES]
            v_bd = bd(vc[:, cs])
            w_ = x[0:CHUNK] + _dot(l_ak.astype(BF16), v_bd)
            u = _dot(tm.astype(BF16), bd(w_))
            u_bd = bd(u)
            y = x[CHUNK:2 * CHUNK] + _dot(
                jnp.concatenate([a_rb.astype(BF16), a_rk.astype(BF16)], axis=1),
                jnp.concatenate([u_bd, v_bd], axis=0))
            uv = jnp.concatenate([u.astype(BF16), vc[:, cs].astype(BF16)], axis=0)
            bk = jnp.concatenate([b_e[:, cs], k_e[:, cs]], axis=0)
            upd = _dot_tn(uv, bk)
            sbd_ref[p] = s_old * dec_all[:, cs] + jnp.where(same_head, upd, 0.0)
            mean = _dot(y.astype(BF16), ones_bd) * (1.0 / HEAD_N)
            yc = y - mean
            var = _dot((yc * yc).astype(BF16), ones_bd) * (1.0 / HEAD_N)
            y_s[rows, cs] = yc * lax.rsqrt(var + GN_EPS)
        return carry

    lax.fori_loop(0, rows_s // CHUNK, chunk_body, 0)

    y_n = y_s[0:tt, :] * lng_ref[...] + lnb_ref[...]
    ob_ref[0] = _dot(((y_n + bo_s[...]) * g_s[...]).astype(BF16), wout_ref[...])

    @pl.when(ti == pl.num_programs(1) - 1)
    def _():
        sout_ref[0] = sbd_ref[...]


def _rwkv_branch(proj, prev0, s0, mu, dbase, wdec, ibase, wiclr, wgate, kkw, ka, rk, lng, lnb, wout,
                 *, tt, t_valid):
    b, t, _ = proj.shape
    d = wout.shape[0]
    n_lora = wgate.shape[0] * 2
    n_pairs = d // LANES
    rows_s = max(tt, CHUNK)
    n_eff = min(CHUNK, t_valid)
    n_sq = max(0, math.ceil(math.log2(n_eff)) - 1)
    kern = functools.partial(_rwkv_kernel, tt=tt, t_valid=t_valid, n_sq=n_sq)
    const = lambda shape: pl.BlockSpec(shape, lambda i, j: (0,) * len(shape))
    col = lambda c: pl.BlockSpec((1, tt, d), lambda i, j: (i, j, c))
    big = lambda: pltpu.VMEM((rows_s, d), F32)
    return pl.pallas_call(
        kern,
        grid=(b, t // tt),
        in_specs=[col(4), col(5), col(6),
                  pl.BlockSpec((1, tt, n_lora), lambda i, j: (i, j, 7 * d // n_lora)),
                  pl.BlockSpec((1, 1, 3 * d + n_lora), lambda i, j: (i, 0, 0)),
                  pl.BlockSpec((1, n_pairs, LANES, LANES), lambda i, j: (i, 0, 0, 0)),
                  const((1, 3 * d + n_lora)), const((1, d)), const((LANES, d)), const((1, d)),
                  const((LANES, d)), const((LANES, d)), const((1, d)), const((1, d)), const((1, d)),
                  const((1, d)), const((1, d)), const((d, d))],
        out_specs=[pl.BlockSpec((1, tt, d), lambda i, j: (i, j, 0)),
                   pl.BlockSpec((1, n_pairs, LANES, LANES), lambda i, j: (i, 0, 0, 0))],
        out_shape=[jax.ShapeDtypeStruct((b, t, d), F32),
                   jax.ShapeDtypeStruct((b, n_pairs, LANES, LANES), F32)],
        scratch_shapes=[pltpu.VMEM((8 + tt, 3 * d + n_lora), F32),
                        big(), big(), big(), big(), big(), big(),
                        pltpu.VMEM((tt, d), F32), pltpu.VMEM((tt, d), F32), big(),
                        pltpu.VMEM((n_pairs, LANES, LANES), F32)],
        compiler_params=_cparams(2),
        name="rwkv_branch",
    )(proj, proj, proj, proj, prev0, s0, mu, dbase, wdec, ibase, wiclr, wgate, kkw, ka, rk, lng, lnb, wout)


def _ffn_kernel(x_ref, ga_ref, gb_ref, oa_ref, ob_ref, wo_ref, g1_ref, g2_ref, wu_ref, wd_ref, g3_ref,
                o_ref, *, ff_chunk):
    merged = _sigmoid(ga_ref[...]) * oa_ref[...] + _sigmoid(gb_ref[...]) * ob_ref[...]
    m = _dot(merged.astype(BF16), wo_ref[...])
    x1 = x_ref[...] + _rms(m, g1_ref[...])
    h2 = _rms(x1, g2_ref[...]).astype(BF16)
    f = None
    for c0 in range(0, wu_ref.shape[1], ff_chunk):
        up = jnp.maximum(_dot(h2, wu_ref[:, c0:c0 + ff_chunk]), 0.0)
        part = _dot((up * up).astype(BF16), wd_ref[c0:c0 + ff_chunk, :])
        f = part if f is None else f + part
    o_ref[...] = x1 + _rms(f, g3_ref[...])


def _merge_ffn(x, proj, o_a, o_b, w_out, g1, g2, w_up, w_down, g3, *, tm):
    m, d = x.shape
    dff = w_up.shape[1]
    tok = lambda c: pl.BlockSpec((tm, d), lambda i: (i, c))
    const = lambda shape: pl.BlockSpec(shape, lambda i: (0, 0), pipeline_mode=pl.Buffered(1))
    return pl.pallas_call(
        functools.partial(_ffn_kernel, ff_chunk=d),
        grid=(m // tm,),
        in_specs=[tok(0), tok(2), tok(3), tok(0), tok(0),
                  const((d, d)), const((1, d)), const((1, d)), const((d, dff)), const((dff, d)),
                  const((1, d))],
        out_specs=tok(0),
        out_shape=jax.ShapeDtypeStruct((m, d), F32),
        compiler_params=_cparams(1),
        name="merge_ffn",
    )(x, proj, proj, o_a, o_b, w_out, g1, g2, w_up, w_down, g3)


def _layer(x, conv_state, prev0, s0, w, *, t_valid, tt, tm, tm_in):
    b, t, d = x.shape
    x2 = x.reshape(b * t, d)
    tm, tm_in = min(tm, b * t), min(tm_in, b * t)
    proj = _norm_matmul(x2, w["pre_mix_g"], w["w_in"], tm=tm_in, tn=w["w_in"].shape[1] // 2,
                        apply_norm=True, name="in_proj")
    proj3 = proj.reshape(b, t, -1)
    o_a, tail = _conv_branch(proj3, conv_state, w["conv_w"], w["conv_b"], w["conv_ln_g"], w["conv_ln_b"],
                             w["w_conv_out"], tt=tt, t_valid=min(t_valid, tt))
    o_b, s_new = _rwkv_branch(proj3, prev0, s0, w["mu"], w["decay_base"], w["w_dec"], w["iclr_base"],
                              w["w_iclr"], w["w_gate"], w["k_k"], w["k_a"], w["r_k"], w["lnx_g"],
                              w["lnx_b"], w["w_rwkv_out"], tt=tt, t_valid=min(t_valid, tt))
    y = _merge_ffn(x2, proj, o_a.reshape(b * t, d), o_b.reshape(b * t, d), w["w_out"], w["post_mix_g"],
                   w["pre_ffn_g"], w["w_ff_up"], w["w_ff_down"], w["post_ffn_g"], tm=tm)
    return y.reshape(b, t, d), tail, s_new


def _state_to_blockdiag(s):
    b, h, n, _ = s.shape
    s = s.reshape(b, h // 2, 2, n, n)
    z = jnp.zeros_like(s[:, :, 0])
    top = jnp.concatenate([s[:, :, 0], z], axis=-1)
    bot = jnp.concatenate([z, s[:, :, 1]], axis=-1)
    return jnp.concatenate([top, bot], axis=-2)


def _blockdiag_to_state(sbd):
    b, p, n2, _ = sbd.shape
    n = n2 // 2
    return jnp.stack([sbd[:, :, :n, :n], sbd[:, :, n:, n:]], axis=2).reshape(b, 2 * p, n, n)


def kernel(x_prompt, x_sample, state_conv, state_shift, state_wkv, pre_mix_g, post_mix_g, pre_ffn_g, post_ffn_g, w_in, conv_w, conv_b, conv_ln_g, conv_ln_b, w_conv_out, shift_mu, decay_base, w_decay_up, iclr_base, w_iclr_up, w_gate_up, k_k, k_a, r_k, lnx_g, lnx_b, w_rwkv_out, w_out, w_ff_up, w_ff_down):
    depth = w_in.shape[0]
    assert depth == 1
    d = x_prompt.shape[-1]
    n_heads = state_wkv.shape[2]
    r_dec, r_iclr, r_gate = w_decay_up.shape[1], w_iclr_up.shape[1], w_gate_up.shape[1]
    assert n_heads * HEAD_N == d and r_dec + r_iclr == LANES and r_gate == LANES
    n_lora = r_dec + r_iclr + r_gate
    i0 = 2 * d
    i1 = i0 + 3 * d + n_lora
    row = lambda a: a.reshape(1, -1)

    w_in_l = w_in[0]
    w_in_p = jnp.concatenate([w_in_l[:, :i0], w_in_l[:, i1:], w_in_l[:, i0:i1]], axis=1).astype(BF16)
    zpad = lambda n: jnp.zeros((n, d), F32)
    w = dict(
        pre_mix_g=row(pre_mix_g[0]), post_mix_g=row(post_mix_g[0]), pre_ffn_g=row(pre_ffn_g[0]),
        post_ffn_g=row(post_ffn_g[0]), w_in=w_in_p,
        conv_w=conv_w[0], conv_b=row(conv_b[0]), conv_ln_g=row(conv_ln_g[0]), conv_ln_b=row(conv_ln_b[0]),
        w_conv_out=w_conv_out[0].astype(BF16), mu=row(shift_mu[0]),
        decay_base=row(decay_base[0]), iclr_base=row(iclr_base[0]),
        w_dec=jnp.concatenate([w_decay_up[0], zpad(r_iclr)], axis=0).astype(BF16),
        w_iclr=jnp.concatenate([zpad(r_dec), w_iclr_up[0]], axis=0).astype(BF16),
        w_gate=w_gate_up[0].astype(BF16),
        k_k=row(k_k[0]), k_a=row(k_a[0]), r_k=row(r_k[0]), lnx_g=row(lnx_g[0]), lnx_b=row(lnx_b[0]),
        w_rwkv_out=w_rwkv_out[0].astype(BF16), w_out=w_out[0].astype(BF16),
        w_ff_up=w_ff_up[0].astype(BF16), w_ff_down=w_ff_down[0].astype(BF16),
    )

    bp, tp, _ = x_prompt.shape
    n_pairs = n_heads // 2
    yp, conv_p, s_p = _layer(
        x_prompt, jnp.zeros((bp, conv_w.shape[1] - 1, d), F32), jnp.zeros((bp, 1, 3 * d + n_lora), F32),
        jnp.zeros((bp, n_pairs, LANES, LANES), F32), w, t_valid=tp, tt=256, tm=256, tm_in=512)
    shift_p = _rms_rows(x_prompt[:, -1], w["pre_mix_g"])

    bs, ts, _ = x_sample.shape
    t_pad = 8
    xs = jnp.pad(x_sample, ((0, 0), (0, t_pad - ts), (0, 0)))
    prev0 = _norm_matmul(state_shift[0], w["pre_mix_g"], w_in_p[:, 4 * d:], tm=bs, tn=3 * d + n_lora,
                         apply_norm=False, name="shift_proj")
    ys, conv_s, s_s = _layer(
        xs, state_conv[0], prev0.reshape(bs, 1, -1), _state_to_blockdiag(state_wkv[0]), w,
        t_valid=ts, tt=t_pad, tm=256, tm_in=512)
    shift_s = _rms_rows(x_sample[:, -1], w["pre_mix_g"])

    return (yp, ys[:, :ts], conv_p[None], shift_p[None], _blockdiag_to_state(s_p)[None],
            conv_s[None], shift_s[None], _blockdiag_to_state(s_s)[None])
```

```python
import functools
import math

import jax
import jax.numpy as jnp
from jax import lax
from jax.experimental import pallas as pl
from jax.experimental.pallas import tpu as pltpu

F32 = jnp.float32
BF16 = jnp.bfloat16

RMS_EPS = 1e-6
LN_EPS = 1e-5
GN_EPS = 64e-5
HEAD_N = 64
LANES = 128
CHUNK = 64
CONV_HALO = 32
VMEM_LIMIT = 52 * 1024 * 1024


def _cparams(n_axes):
    return pltpu.CompilerParams(dimension_semantics=("arbitrary",) * n_axes,
                                vmem_limit_bytes=VMEM_LIMIT)


def _rms(x, g):
    return x * lax.rsqrt(jnp.mean(x * x, axis=-1, keepdims=True) + RMS_EPS) * g


def _sigmoid(x):
    return 0.5 * jnp.tanh(0.5 * x) + 0.5


def _dot(a, b):
    return jnp.dot(a, b, preferred_element_type=F32)


def _dot_nt(a, b):
    return lax.dot_general(a, b, (((1,), (1,)), ((), ())), preferred_element_type=F32)


def _dot_tn(a, b):
    return lax.dot_general(a, b, (((0,), (0,)), ((), ())), preferred_element_type=F32)


def _proj_kernel(x_ref, g_ref, w_ref, o_ref, *, apply_norm):
    x = x_ref[...]
    if apply_norm:
        x = _rms(x, g_ref[...])
    o_ref[...] = _dot(x.astype(BF16), w_ref[...])


def _norm_matmul(x, g, w, *, tm, tn, apply_norm, name):
    m, d = x.shape
    n = w.shape[1]
    assert m % tm == 0 and n % tn == 0
    return pl.pallas_call(
        functools.partial(_proj_kernel, apply_norm=apply_norm),
        grid=(n // tn, m // tm),
        in_specs=[pl.BlockSpec((tm, d), lambda j, i: (i, 0)),
                  pl.BlockSpec((1, d), lambda j, i: (0, 0)),
                  pl.BlockSpec((d, tn), lambda j, i: (0, j))],
        out_specs=pl.BlockSpec((tm, tn), lambda j, i: (i, j)),
        out_shape=jax.ShapeDtypeStruct((m, n), F32),
        compiler_params=_cparams(2),
        name=name,
    )(x, g, w)


def _rms_rows_kernel(x_ref, g_ref, o_ref):
    o_ref[...] = _rms(x_ref[...], g_ref[...])


def _rms_rows(x, g):
    return pl.pallas_call(_rms_rows_kernel, out_shape=jax.ShapeDtypeStruct(x.shape, F32),
                          name="shift_rows")(x, g)


def _conv_kernel(val_ref, gate_ref, st_ref, cw_ref, cb_ref, lg_ref, lb_ref, wo_ref,
                 oa_ref, tail_ref, ext_ref, c_ref, *, tt, t_valid, rb, n_taps):
    halo = n_taps - 1
    lead = CONV_HALO - halo

    @pl.when(pl.program_id(1) == 0)
    def _():
        ext_ref[0:lead, :] = jnp.zeros((lead, ext_ref.shape[1]), F32)
        ext_ref[lead:CONV_HALO, :] = st_ref[0]

    u = val_ref[0] * _sigmoid(gate_ref[0])
    ext_ref[CONV_HALO:CONV_HALO + tt, :] = u
    for r0 in range(0, tt, rb):
        acc = jnp.broadcast_to(cb_ref[...], (rb, cb_ref.shape[1]))
        for j in range(n_taps):
            acc = acc + ext_ref[lead + j + r0:lead + j + r0 + rb, :] * cw_ref[j:j + 1, :]
        c_ref[r0:r0 + rb, :] = acc
    c = c_ref[...]
    mu = jnp.mean(c, axis=-1, keepdims=True)
    cc = c - mu
    var = jnp.mean(cc * cc, axis=-1, keepdims=True)
    z = cc * lax.rsqrt(var + LN_EPS) * lg_ref[...] + lb_ref[...]
    z = z * _sigmoid(z)
    oa_ref[0] = _dot(z.astype(BF16), wo_ref[...])
    tail = ext_ref[lead + t_valid:lead + t_valid + halo, :]
    ext_ref[lead:CONV_HALO, :] = tail
    tail_ref[0] = tail


def _conv_branch(proj, conv_state, conv_w, conv_b, ln_g, ln_b, w_conv_out, *, tt, t_valid):
    b, t, _ = proj.shape
    d = conv_w.shape[1]
    n_taps = conv_w.shape[0]
    rb = min(tt, 32)
    kern = functools.partial(_conv_kernel, tt=tt, t_valid=t_valid, rb=rb, n_taps=n_taps)
    row = lambda shape: pl.BlockSpec(shape, lambda i, j: (0, 0))
    return pl.pallas_call(
        kern,
        grid=(b, t // tt),
        in_specs=[pl.BlockSpec((1, tt, d), lambda i, j: (i, j, 0)),
                  pl.BlockSpec((1, tt, d), lambda i, j: (i, j, 1)),
                  pl.BlockSpec((1, n_taps - 1, d), lambda i, j: (i, 0, 0)),
                  row((n_taps, d)), row((1, d)), row((1, d)), row((1, d)), row((d, d))],
        out_specs=[pl.BlockSpec((1, tt, d), lambda i, j: (i, j, 0)),
                   pl.BlockSpec((1, n_taps - 1, d), lambda i, j: (i, 0, 0))],
        out_shape=[jax.ShapeDtypeStruct((b, t, d), F32),
                   jax.ShapeDtypeStruct((b, n_taps - 1, d), F32)],
        scratch_shapes=[pltpu.VMEM((CONV_HALO + tt, d), F32), pltpu.VMEM((tt, d), F32)],
        compiler_params=_cparams(2),
        name="conv_branch",
    )(proj, proj, conv_state, conv_w, conv_b, ln_g, ln_b, w_conv_out)


def _split3(x):
    hi = x.astype(BF16)
    r1 = x - hi.astype(F32)
    mid = r1.astype(BF16)
    lo = (r1 - mid.astype(F32)).astype(BF16)
    return hi, mid, lo


def _rwkv_kernel(r_ref, k_ref, v_ref, lo_ref, prev_ref, s0_ref,
                 mu_ref, dbase_ref, wdec_ref, ibase_ref, wiclr_ref, wgate_ref,
                 kkw_ref, ka_ref, rk_ref, lng_ref, lnb_ref, wout_ref,
                 ob_ref, sout_ref,
                 sh_ref, at_b, rt_b, bh_b, kh_b, be_b, ke_b, v_b, dall_s, g_s, bo_s,
                 tm_s, arb_s, lv_s, yv_s, y_s, sbd_ref,
                 *, tt, t_valid, n_sq, group):
    d = r_ref.shape[2]
    n_lora = lo_ref.shape[2]
    n_pairs = d // LANES
    rows_s = at_b.shape[0]
    n_chunks = rows_s // CHUNK
    ti = pl.program_id(1)

    @pl.when(ti == 0)
    def _():
        sh_ref[7:8, :] = prev_ref[0]
        sbd_ref[...] = s0_ref[0]

    sh_ref[8:8 + tt, 0:d] = r_ref[0]
    sh_ref[8:8 + tt, d:2 * d] = k_ref[0]
    sh_ref[8:8 + tt, 2 * d:3 * d] = v_ref[0]
    sh_ref[8:8 + tt, 3 * d:3 * d + n_lora] = lo_ref[0]

    def mixed(c0, c1):
        cur = sh_ref[8:8 + tt, c0:c1]
        prv = sh_ref[7:7 + tt, c0:c1]
        return cur + (prv - cur) * mu_ref[:, c0:c1]

    r = mixed(0, d)
    k = mixed(d, 2 * d)
    v = mixed(2 * d, 3 * d)
    lora = mixed(3 * d, 3 * d + n_lora)
    sh_ref[7:8, :] = sh_ref[7 + t_valid:8 + t_valid, :]

    l_di = lora[:, 0:LANES]
    l_g = lora[:, LANES:2 * LANES]
    dec_in = dbase_ref[...] + _dot(jnp.tanh(l_di).astype(BF16), wdec_ref[...])
    a = _sigmoid(ibase_ref[...] + _dot(l_di.astype(BF16), wiclr_ref[...]))
    g = _dot(_sigmoid(l_g).astype(BF16), wgate_ref[...])
    z = -dec_in
    softplus = jnp.maximum(z, 0.0) + jnp.log(1.0 + jnp.exp(-jnp.abs(z)))
    lw = -jnp.exp(-softplus - 0.5)

    li = lax.broadcasted_iota(jnp.int32, (LANES, LANES), 0) // HEAD_N
    lj = lax.broadcasted_iota(jnp.int32, (LANES, LANES), 1) // HEAD_N
    same_head = li == lj
    ones_bd = jnp.where(same_head, 1.0, 0.0).astype(BF16)

    def head_sum(x):
        parts = [_dot(x[:, p * LANES:(p + 1) * LANES].astype(BF16), ones_bd) for p in range(n_pairs)]
        return jnp.concatenate(parts, axis=1)

    kk = k * kkw_ref[...]
    kk = kk * lax.rsqrt(jnp.maximum(head_sum(kk * kk), 1e-24))
    kt = k * (1.0 + (a - 1.0) * ka_ref[...])
    bonus = head_sum(r * kt * rk_ref[...]) * v
    alpha = -kk
    beta = kk * a
    if t_valid < tt:
        live = lax.broadcasted_iota(jnp.int32, (tt, d), 0) < t_valid
        zero = jnp.zeros((tt, d), F32)
        lw, kt, v, alpha, beta = (jnp.where(live, x, zero) for x in (lw, kt, v, alpha, beta))
    g_s[...] = g
    bo_s[...] = bonus

    ri = lax.broadcasted_iota(jnp.int32, (tt, tt), 0)
    rj = lax.broadcasted_iota(jnp.int32, (tt, tt), 1)
    same_chunk = (ri // CHUNK) == (rj // CHUNK)
    ones_blk = jnp.where(same_chunk, 1.0, 0.0)
    tril_blk = jnp.where(rj <= ri, ones_blk, 0.0).astype(BF16)
    ones_blk = ones_blk.astype(BF16)
    hi, mid, lo = _split3(lw)
    cum = _dot(tril_blk, hi) + _dot(tril_blk, mid) + _dot(tril_blk, lo)
    tot = _dot(ones_blk, hi) + _dot(ones_blk, mid) + _dot(ones_blk, lo)
    dec_in_ = jnp.exp(cum)
    dec_inv = jnp.exp(-cum)
    dec_prev = jnp.exp(cum - lw)
    dec_tail = jnp.exp(tot - cum)
    dec_all = jnp.exp(tot)

    def put(ref, val):
        if tt < rows_s:
            val = jnp.concatenate([val, jnp.zeros((rows_s - tt, d), F32)], axis=0)
        ref[...] = val.astype(BF16)

    put(at_b, alpha * dec_prev)
    put(rt_b, r * dec_in_)
    put(bh_b, beta * dec_inv)
    put(kh_b, kt * dec_inv)
    put(be_b, beta * dec_tail)
    put(ke_b, kt * dec_tail)
    put(v_b, v)
    for c in range(n_chunks):
        dall_s[c:c + 1, :] = dec_all[c * CHUNK:c * CHUNK + 1, :]

    pt = lax.broadcasted_iota(jnp.int32, (CHUNK, LANES), 0)
    pl_ = lax.broadcasted_iota(jnp.int32, (CHUNK, LANES), 1)
    ps = pl_ % HEAD_N
    strict = ps < pt
    incl = ps <= pt
    lane_lo = pl_ < HEAD_N
    eye_pair = jnp.where(ps == pt, 1.0, 0.0)
    zc = jnp.zeros((CHUNK, LANES), F32)

    m_lo = jnp.where(lane_lo, 1.0, 0.0).astype(BF16)
    m_hi = jnp.where(lane_lo, 0.0, 1.0).astype(BF16)

    def bd(xb):
        return jnp.concatenate([xb * m_lo, xb * m_hi], axis=0)

    def blk(ref, c, p):
        return ref[c * CHUNK:(c + 1) * CHUNK, p * LANES:(p + 1) * LANES]

    def put_blk(ref, c, p, val):
        ref[c * CHUNK:(c + 1) * CHUNK, p * LANES:(p + 1) * LANES] = val

    items = [(c, p) for c in range(n_chunks) for p in range(n_pairs)]
    for g0 in range(0, len(items), group):
        its = items[g0:g0 + group]
        ar = [_dot_nt(jnp.concatenate([blk(at_b, c, p), blk(rt_b, c, p)], axis=0),
                      jnp.concatenate([bd(blk(bh_b, c, p)), bd(blk(kh_b, c, p))], axis=0))
              for c, p in its]
        l_ab = [jnp.where(strict, a[0:CHUNK, 0:LANES], zc) for a in ar]
        for (c, p), a in zip(its, ar):
            put_blk(arb_s, c, p, jnp.where(incl, a[CHUNK:2 * CHUNK, 0:LANES], zc).astype(BF16))
        lv_yv = [_dot(jnp.concatenate([jnp.where(strict, a[0:CHUNK, LANES:2 * LANES], zc),
                                       jnp.where(incl, a[CHUNK:2 * CHUNK, LANES:2 * LANES], zc)],
                                      axis=0).astype(BF16), bd(blk(v_b, c, p)))
                 for (c, p), a in zip(its, ar)]
        for (c, p), m in zip(its, lv_yv):
            put_blk(lv_s, c, p, m[0:CHUNK])
            put_blk(yv_s, c, p, m[CHUNK:2 * CHUNK])
        tm = [eye_pair + l for l in l_ab]
        if n_sq > 0:
            pw = [_dot(l.astype(BF16), bd(l.astype(BF16))) for l in l_ab]
        for i in range(n_sq):
            if i == n_sq - 1:
                tm = [t + _dot(q.astype(BF16), bd(t.astype(BF16))) for t, q in zip(tm, pw)]
            else:
                pr = [_dot(q.astype(BF16), jnp.concatenate([bd(q.astype(BF16)), bd(t.astype(BF16))], axis=1))
                      for t, q in zip(tm, pw)]
                pw = [x[:, 0:LANES] for x in pr]
                tm = [t + x[:, LANES:2 * LANES] for t, x in zip(tm, pr)]
        for (c, p), t in zip(its, tm):
            put_blk(tm_s, c, p, t.astype(BF16))

    pairs = range(n_pairs)
    for c in range(n_chunks):
        s_old = [sbd_ref[p] for p in pairs]
        x = [_dot_nt(jnp.concatenate([blk(at_b, c, p), blk(rt_b, c, p)], axis=0), s_old[p].astype(BF16))
             for p in pairs]
        ub = [_dot(blk(tm_s, c, p), bd((x[p][0:CHUNK] + blk(lv_s, c, p)).astype(BF16))).astype(BF16)
              for p in pairs]
        upd = [_dot_tn(jnp.concatenate([ub[p], blk(v_b, c, p)], axis=0),
                       jnp.concatenate([blk(be_b, c, p), blk(ke_b, c, p)], axis=0)) for p in pairs]
        for p in pairs:
            sbd_ref[p] = (s_old[p] * dall_s[c:c + 1, p * LANES:(p + 1) * LANES]
                          + jnp.where(same_head, upd[p], 0.0))
            put_blk(y_s, c, p, x[p][CHUNK:2 * CHUNK] + blk(yv_s, c, p) + _dot(blk(arb_s, c, p), bd(ub[p])))

    y_n = []
    for p in pairs:
        yb = y_s[0:tt, p * LANES:(p + 1) * LANES]
        yc = yb - _dot(yb.astype(BF16), ones_bd) * (1.0 / HEAD_N)
        var = _dot((yc * yc).astype(BF16), ones_bd) * (1.0 / HEAD_N)
        y_n.append(yc * lax.rsqrt(var + GN_EPS))
    y_n = jnp.concatenate(y_n, axis=1) * lng_ref[...] + lnb_ref[...]
    ob_ref[0] = _dot(((y_n + bo_s[...]) * g_s[...]).astype(BF16), wout_ref[...])

    @pl.when(ti == pl.num_programs(1) - 1)
    def _():
        sout_ref[0] = sbd_ref[...]


def _rwkv_branch(proj, prev0, s0, mu, dbase, wdec, ibase, wiclr, wgate, kkw, ka, rk, lng, lnb, wout,
                 *, tt, t_valid):
    b, t, _ = proj.shape
    d = wout.shape[0]
    n_lora = wgate.shape[0] * 2
    n_pairs = d // LANES
    rows_s = max(tt, CHUNK)
    n_eff = min(CHUNK, t_valid)
    n_sq = max(0, math.ceil(math.log2(n_eff)) - 1)
    kern = functools.partial(_rwkv_kernel, tt=tt, t_valid=t_valid, n_sq=n_sq, group=2 * n_pairs)
    const = lambda shape: pl.BlockSpec(shape, lambda i, j: (0,) * len(shape))
    col = lambda c: pl.BlockSpec((1, tt, d), lambda i, j: (i, j, c))
    big = lambda dt: pltpu.VMEM((rows_s, d), dt)
    return pl.pallas_call(
        kern,
        grid=(b, t // tt),
        in_specs=[col(4), col(5), col(6),
                  pl.BlockSpec((1, tt, n_lora), lambda i, j: (i, j, 7 * d // n_lora)),
                  pl.BlockSpec((1, 1, 3 * d + n_lora), lambda i, j: (i, 0, 0)),
                  pl.BlockSpec((1, n_pairs, LANES, LANES), lambda i, j: (i, 0, 0, 0)),
                  const((1, 3 * d + n_lora)), const((1, d)), const((LANES, d)), const((1, d)),
                  const((LANES, d)), const((LANES, d)), const((1, d)), const((1, d)), const((1, d)),
                  const((1, d)), const((1, d)), const((d, d))],
        out_specs=[pl.BlockSpec((1, tt, d), lambda i, j: (i, j, 0)),
                   pl.BlockSpec((1, n_pairs, LANES, LANES), lambda i, j: (i, 0, 0, 0))],
        out_shape=[jax.ShapeDtypeStruct((b, t, d), F32),
                   jax.ShapeDtypeStruct((b, n_pairs, LANES, LANES), F32)],
        scratch_shapes=[pltpu.VMEM((8 + tt, 3 * d + n_lora), F32),
                        big(BF16), big(BF16), big(BF16), big(BF16), big(BF16), big(BF16), big(BF16),
                        pltpu.VMEM((max(rows_s // CHUNK, 8), d), F32),
                        pltpu.VMEM((tt, d), F32), pltpu.VMEM((tt, d), F32),
                        big(BF16), big(BF16), big(F32), big(F32), big(F32),
                        pltpu.VMEM((n_pairs, LANES, LANES), F32)],
        compiler_params=_cparams(2),
        name="rwkv_branch",
    )(proj, proj, proj, proj, prev0, s0, mu, dbase, wdec, ibase, wiclr, wgate, kkw, ka, rk, lng, lnb, wout)


def _ffn_kernel(x_ref, ga_ref, gb_ref, oa_ref, ob_ref, wo_ref, g1_ref, g2_ref, wu_ref, wd_ref, g3_ref,
                o_ref, *, ff_chunk):
    merged = _sigmoid(ga_ref[...]) * oa_ref[...] + _sigmoid(gb_ref[...]) * ob_ref[...]
    m = _dot(merged.astype(BF16), wo_ref[...])
    x1 = x_ref[...] + _rms(m, g1_ref[...])
    h2 = _rms(x1, g2_ref[...]).astype(BF16)
    f = None
    for c0 in range(0, wu_ref.shape[1], ff_chunk):
        up = jnp.maximum(_dot(h2, wu_ref[:, c0:c0 + ff_chunk]), 0.0)
        part = _dot((up * up).astype(BF16), wd_ref[c0:c0 + ff_chunk, :])
        f = part if f is None else f + part
    o_ref[...] = x1 + _rms(f, g3_ref[...])


def _merge_ffn(x, proj, o_a, o_b, w_out, g1, g2, w_up, w_down, g3, *, tm):
    m, d = x.shape
    dff = w_up.shape[1]
    tok = lambda c: pl.BlockSpec((tm, d), lambda i: (i, c))
    const = lambda shape: pl.BlockSpec(shape, lambda i: (0, 0), pipeline_mode=pl.Buffered(1))
    return pl.pallas_call(
        functools.partial(_ffn_kernel, ff_chunk=d),
        grid=(m // tm,),
        in_specs=[tok(0), tok(2), tok(3), tok(0), tok(0),
                  const((d, d)), const((1, d)), const((1, d)), const((d, dff)), const((dff, d)),
                  const((1, d))],
        out_specs=tok(0),
        out_shape=jax.ShapeDtypeStruct((m, d), F32),
        compiler_params=_cparams(1),
        name="merge_ffn",
    )(x, proj, proj, o_a, o_b, w_out, g1, g2, w_up, w_down, g3)


def _layer(x, conv_state, prev0, s0, w, *, t_valid, tt, tm, tm_in):
    b, t, d = x.shape
    x2 = x.reshape(b * t, d)
    tm, tm_in = min(tm, b * t), min(tm_in, b * t)
    proj = _norm_matmul(x2, w["pre_mix_g"], w["w_in"], tm=tm_in, tn=w["w_in"].shape[1] // 2,
                        apply_norm=True, name="in_proj")
    proj3 = proj.reshape(b, t, -1)
    o_a, tail = _conv_branch(proj3, conv_state, w["conv_w"], w["conv_b"], w["conv_ln_g"], w["conv_ln_b"],
                             w["w_conv_out"], tt=tt, t_valid=min(t_valid, tt))
    o_b, s_new = _rwkv_branch(proj3, prev0, s0, w["mu"], w["decay_base"], w["w_dec"], w["iclr_base"],
                              w["w_iclr"], w["w_gate"], w["k_k"], w["k_a"], w["r_k"], w["lnx_g"],
                              w["lnx_b"], w["w_rwkv_out"], tt=tt, t_valid=min(t_valid, tt))
    y = _merge_ffn(x2, proj, o_a.reshape(b * t, d), o_b.reshape(b * t, d), w["w_out"], w["post_mix_g"],
                   w["pre_ffn_g"], w["w_ff_up"], w["w_ff_down"], w["post_ffn_g"], tm=tm)
    return y.reshape(b, t, d), tail, s_new


def _state_to_blockdiag(s):
    b, h, n, _ = s.shape
    s = s.reshape(b, h // 2, 2, n, n)
    z = jnp.zeros_like(s[:, :, 0])
    top = jnp.concatenate([s[:, :, 0], z], axis=-1)
    bot = jnp.concatenate([z, s[:, :, 1]], axis=-1)
    return jnp.concatenate([top, bot], axis=-2)


def _blockdiag_to_state(sbd):
    b, p, n2, _ = sbd.shape
    n = n2 // 2
    return jnp.stack([sbd[:, :, :n, :n], sbd[:, :, n:, n:]], axis=2).reshape(b, 2 * p, n, n)


def kernel(x_prompt, x_sample, state_conv, state_shift, state_wkv, pre_mix_g, post_mix_g, pre_ffn_g, post_ffn_g, w_in, conv_w, conv_b, conv_ln_g, conv_ln_b, w_conv_out, shift_mu, decay_base, w_decay_up, iclr_base, w_iclr_up, w_gate_up, k_k, k_a, r_k, lnx_g, lnx_b, w_rwkv_out, w_out, w_ff_up, w_ff_down):
    depth = w_in.shape[0]
    assert depth == 1
    d = x_prompt.shape[-1]
    n_heads = state_wkv.shape[2]
    r_dec, r_iclr, r_gate = w_decay_up.shape[1], w_iclr_up.shape[1], w_gate_up.shape[1]
    assert n_heads * HEAD_N == d and r_dec + r_iclr == LANES and r_gate == LANES
    n_lora = r_dec + r_iclr + r_gate
    i0 = 2 * d
    i1 = i0 + 3 * d + n_lora
    row = lambda a: a.reshape(1, -1)

    w_in_l = w_in[0]
    w_in_p = jnp.concatenate([w_in_l[:, :i0], w_in_l[:, i1:], w_in_l[:, i0:i1]], axis=1).astype(BF16)
    zpad = lambda n: jnp.zeros((n, d), F32)
    w = dict(
        pre_mix_g=row(pre_mix_g[0]), post_mix_g=row(post_mix_g[0]), pre_ffn_g=row(pre_ffn_g[0]),
        post_ffn_g=row(post_ffn_g[0]), w_in=w_in_p,
        conv_w=conv_w[0], conv_b=row(conv_b[0]), conv_ln_g=row(conv_ln_g[0]), conv_ln_b=row(conv_ln_b[0]),
        w_conv_out=w_conv_out[0].astype(BF16), mu=row(shift_mu[0]),
        decay_base=row(decay_base[0]), iclr_base=row(iclr_base[0]),
        w_dec=jnp.concatenate([w_decay_up[0], zpad(r_iclr)], axis=0).astype(BF16),
        w_iclr=jnp.concatenate([zpad(r_dec), w_iclr_up[0]], axis=0).astype(BF16),
        w_gate=w_gate_up[0].astype(BF16),
        k_k=row(k_k[0]), k_a=row(k_a[0]), r_k=row(r_k[0]), lnx_g=row(lnx_g[0]), lnx_b=row(lnx_b[0]),
        w_rwkv_out=w_rwkv_out[0].astype(BF16), w_out=w_out[0].astype(BF16),
        w_ff_up=w_ff_up[0].astype(BF16), w_ff_down=w_ff_down[0].astype(BF16),
    )

    bp, tp, _ = x_prompt.shape
    n_pairs = n_heads // 2
    yp, conv_p, s_p = _layer(
        x_prompt, jnp.zeros((bp, conv_w.shape[1] - 1, d), F32), jnp.zeros((bp, 1, 3 * d + n_lora), F32),
        jnp.zeros((bp, n_pairs, LANES, LANES), F32), w, t_valid=tp, tt=256, tm=256, tm_in=512)
    shift_p = _rms_rows(x_prompt[:, -1], w["pre_mix_g"])

    bs, ts, _ = x_sample.shape
    t_pad = 8
    xs = jnp.pad(x_sample, ((0, 0), (0, t_pad - ts), (0, 0)))
    prev0 = _norm_matmul(state_shift[0], w["pre_mix_g"], w_in_p[:, 4 * d:], tm=bs, tn=3 * d + n_lora,
                         apply_norm=False, name="shift_proj")
    ys, conv_s, s_s = _layer(
        xs, state_conv[0], prev0.reshape(bs, 1, -1), _state_to_blockdiag(state_wkv[0]), w,
        t_valid=ts, tt=t_pad, tm=256, tm_in=512)
    shift_s = _rms_rows(x_sample[:, -1], w["pre_mix_g"])

    return (yp, ys[:, :ts], conv_p[None], shift_p[None], _blockdiag_to_state(s_p)[None],
            conv_s[None], shift_s[None], _blockdiag_to_state(s_s)[None])
```

```python
import functools
import math

import jax
import jax.numpy as jnp
from jax import lax
from jax.experimental import pallas as pl
from jax.experimental.pallas import tpu as pltpu

F32 = jnp.float32
BF16 = jnp.bfloat16

RMS_EPS = 1e-6
LN_EPS = 1e-5
GN_EPS = 64e-5
HEAD_N = 64
LANES = 128
SUBLANES = 8
MXU_N = 256
CHUNK = 64
CONV_HALO = 32
VMEM_LIMIT = 52 * 1024 * 1024


def _cparams(n_axes):
    return pltpu.CompilerParams(dimension_semantics=("arbitrary",) * n_axes,
                                vmem_limit_bytes=VMEM_LIMIT)


def _rms(x, g):
    return x * lax.rsqrt(jnp.mean(x * x, axis=-1, keepdims=True) + RMS_EPS) * g


def _sigmoid(x):
    return 0.5 * jnp.tanh(0.5 * x) + 0.5


def _dot(a, b):
    return jnp.dot(a, b, preferred_element_type=F32)


def _dot_nt(a, b):
    return lax.dot_general(a, b, (((1,), (1,)), ((), ())), preferred_element_type=F32)


def _dot_tn(a, b):
    return lax.dot_general(a, b, (((0,), (0,)), ((), ())), preferred_element_type=F32)


def _proj_kernel(x_ref, g_ref, w_ref, o_ref, *, apply_norm):
    x = x_ref[...]
    if apply_norm:
        x = _rms(x, g_ref[...])
    o_ref[...] = _dot(x.astype(BF16), w_ref[...])


def _norm_matmul(x, g, w, *, tm, tn, apply_norm, name):
    m, d = x.shape
    n = w.shape[1]
    assert m % tm == 0 and n % tn == 0
    return pl.pallas_call(
        functools.partial(_proj_kernel, apply_norm=apply_norm),
        grid=(n // tn, m // tm),
        in_specs=[pl.BlockSpec((tm, d), lambda j, i: (i, 0)),
                  pl.BlockSpec((1, d), lambda j, i: (0, 0)),
                  pl.BlockSpec((d, tn), lambda j, i: (0, j))],
        out_specs=pl.BlockSpec((tm, tn), lambda j, i: (i, j)),
        out_shape=jax.ShapeDtypeStruct((m, n), F32),
        compiler_params=_cparams(2),
        name=name,
    )(x, g, w)


def _rms_rows_kernel(x_ref, g_ref, o_ref):
    o_ref[...] = _rms(x_ref[...], g_ref[...])


def _rms_rows(x, g):
    return pl.pallas_call(_rms_rows_kernel, out_shape=jax.ShapeDtypeStruct(x.shape, F32),
                          name="shift_rows")(x, g)


def _conv_kernel(val_ref, gate_ref, st_ref, cw_ref, cb_ref, lg_ref, lb_ref, wo_ref,
                 oa_ref, tail_ref, ext_ref, carry_ref, c_ref, wb_ref, *, nb, tt, t_valid, rb, n_taps):
    halo = n_taps - 1
    lead = CONV_HALO - halo
    rows = CONV_HALO + tt
    d = cw_ref.shape[1]

    @pl.when(pl.program_id(1) == 0)
    def _():
        carry_ref[...] = st_ref[...]

    @pl.when((pl.program_id(0) == 0) & (pl.program_id(1) == 0))
    def _():
        for j in range(n_taps):
            wb_ref[j * SUBLANES:(j + 1) * SUBLANES, :] = jnp.broadcast_to(cw_ref[j:j + 1, :], (SUBLANES, d))

    for i in range(nb):
        ext_ref[0, 0:lead, :] = jnp.zeros((lead, d), F32)
        ext_ref[0, lead:CONV_HALO, :] = carry_ref[i]
        ext_ref[0, CONV_HALO:rows, :] = val_ref[i] * _sigmoid(gate_ref[i])
        for s in range(1, SUBLANES):
            ext_ref[s, 0:rows - SUBLANES, :] = ext_ref[0, s:s + rows - SUBLANES, :]
        for r0 in range(0, tt, rb):
            acc = jnp.broadcast_to(cb_ref[...], (rb, d))
            for j in range(n_taps):
                q, s = divmod(lead + j, SUBLANES)
                a = q * SUBLANES + r0
                w8 = wb_ref[j * SUBLANES:(j + 1) * SUBLANES, :]
                acc = acc + ext_ref[s, a:a + rb, :] * jnp.concatenate([w8] * (rb // SUBLANES), axis=0)
            c_ref[i * tt + r0:i * tt + r0 + rb, :] = acc
        tail = ext_ref[0, lead + t_valid:lead + t_valid + halo, :]
        carry_ref[i] = tail
        tail_ref[i] = tail
    c = c_ref[...]
    mu = jnp.mean(c, axis=-1, keepdims=True)
    cc = c - mu
    var = jnp.mean(cc * cc, axis=-1, keepdims=True)
    z = cc * lax.rsqrt(var + LN_EPS) * lg_ref[...] + lb_ref[...]
    z = z * _sigmoid(z)
    out = _dot(z.astype(BF16), wo_ref[...])
    for i in range(nb):
        oa_ref[i] = out[i * tt:(i + 1) * tt]


def _conv_branch(proj, conv_state, conv_w, conv_b, ln_g, ln_b, w_conv_out, *, nb, tt, t_valid):
    b, t, _ = proj.shape
    d = conv_w.shape[1]
    n_taps = conv_w.shape[0]
    assert b % nb == 0 and t % tt == 0 and (nb == 1 or t == tt)
    rb = min(tt, 16)
    kern = functools.partial(_conv_kernel, nb=nb, tt=tt, t_valid=t_valid, rb=rb, n_taps=n_taps)
    row = lambda shape: pl.BlockSpec(shape, lambda i, j: (0, 0))
    return pl.pallas_call(
        kern,
        grid=(b // nb, t // tt),
        in_specs=[pl.BlockSpec((nb, tt, d), lambda i, j: (i, j, 0)),
                  pl.BlockSpec((nb, tt, d), lambda i, j: (i, j, 1)),
                  pl.BlockSpec((nb, n_taps - 1, d), lambda i, j: (i, 0, 0)),
                  row((n_taps, d)), row((1, d)), row((1, d)), row((1, d)), row((d, d))],
        out_specs=[pl.BlockSpec((nb, tt, d), lambda i, j: (i, j, 0)),
                   pl.BlockSpec((nb, n_taps - 1, d), lambda i, j: (i, 0, 0))],
        out_shape=[jax.ShapeDtypeStruct((b, t, d), F32),
                   jax.ShapeDtypeStruct((b, n_taps - 1, d), F32)],
        scratch_shapes=[pltpu.VMEM((SUBLANES, CONV_HALO + tt, d), F32),
                        pltpu.VMEM((nb, n_taps - 1, d), F32),
                        pltpu.VMEM((nb * tt, d), F32),
                        pltpu.VMEM((n_taps * SUBLANES, d), F32)],
        compiler_params=_cparams(2),
        name="conv_branch",
    )(proj, proj, conv_state, conv_w, conv_b, ln_g, ln_b, w_conv_out)


def _split3(x):
    hi = x.astype(BF16)
    r1 = x - hi.astype(F32)
    mid = r1.astype(BF16)
    lo = (r1 - mid.astype(F32)).astype(BF16)
    return hi, mid, lo


def _rwkv_kernel(r_ref, k_ref, v_ref, lo_ref, prev_ref, s0_ref,
                 mu_ref, dbase_ref, wdec_ref, ibase_ref, wiclr_ref, wgate_ref,
                 kkw_ref, ka_ref, rk_ref, lng_ref, lnb_ref, wout_ref,
                 ob_ref, sout_ref,
                 sh_ref, at_b, rt_b, bh_b, kh_b, be_b, ke_b, v_b, dall_s, g_s, bo_s,
                 tm_s, arb_s, lv_s, yv_s, y_s, sbd_ref,
                 *, tt, t_valid, n_sq, group):
    d = r_ref.shape[2]
    n_lora = lo_ref.shape[2]
    n_pairs = d // LANES
    rows_s = at_b.shape[0]
    n_chunks = rows_s // CHUNK
    ti = pl.program_id(1)

    @pl.when(ti == 0)
    def _():
        sh_ref[7:8, :] = prev_ref[0]
        zh = jnp.zeros((HEAD_N, HEAD_N), F32)
        for p in range(n_pairs):
            sbd_ref[p] = jnp.concatenate(
                [jnp.concatenate([s0_ref[0, 2 * p], zh], axis=1),
                 jnp.concatenate([zh, s0_ref[0, 2 * p + 1]], axis=1)], axis=0)

    sh_ref[8:8 + tt, 0:d] = r_ref[0]
    sh_ref[8:8 + tt, d:2 * d] = k_ref[0]
    sh_ref[8:8 + tt, 2 * d:3 * d] = v_ref[0]
    sh_ref[8:8 + tt, 3 * d:3 * d + n_lora] = lo_ref[0]

    def mixed(c0, c1):
        cur = sh_ref[8:8 + tt, c0:c1]
        prv = sh_ref[7:7 + tt, c0:c1]
        return cur + (prv - cur) * mu_ref[:, c0:c1]

    r = mixed(0, d)
    k = mixed(d, 2 * d)
    v = mixed(2 * d, 3 * d)
    lora = mixed(3 * d, 3 * d + n_lora)
    sh_ref[7:8, :] = sh_ref[7 + t_valid:8 + t_valid, :]

    l_di = lora[:, 0:LANES]
    l_g = lora[:, LANES:2 * LANES]
    dec_in = dbase_ref[...] + _dot(jnp.tanh(l_di).astype(BF16), wdec_ref[...])
    a = _sigmoid(ibase_ref[...] + _dot(l_di.astype(BF16), wiclr_ref[...]))
    g = _dot(_sigmoid(l_g).astype(BF16), wgate_ref[...])
    z = -dec_in
    softplus = jnp.maximum(z, 0.0) + jnp.log(1.0 + jnp.exp(-jnp.abs(z)))
    lw = -jnp.exp(-softplus - 0.5)

    li = lax.broadcasted_iota(jnp.int32, (LANES, LANES), 0) // HEAD_N
    lj = lax.broadcasted_iota(jnp.int32, (LANES, LANES), 1) // HEAD_N
    same_head = li == lj
    wi = lax.broadcasted_iota(jnp.int32, (MXU_N, MXU_N), 0) // HEAD_N
    wj = lax.broadcasted_iota(jnp.int32, (MXU_N, MXU_N), 1) // HEAD_N
    ones_bd = jnp.where(wi == wj, 1.0, 0.0).astype(BF16)

    def head_sum(x):
        parts = [_dot(x[:, c0:c0 + MXU_N].astype(BF16), ones_bd) for c0 in range(0, d, MXU_N)]
        return jnp.concatenate(parts, axis=1)

    kk = k * kkw_ref[...]
    kk = kk * lax.rsqrt(jnp.maximum(head_sum(kk * kk), 1e-24))
    kt = k * (1.0 + (a - 1.0) * ka_ref[...])
    bonus = head_sum(r * kt * rk_ref[...]) * v
    alpha = -kk
    beta = kk * a
    if t_valid < tt:
        live = lax.broadcasted_iota(jnp.int32, (tt, d), 0) < t_valid
        zero = jnp.zeros((tt, d), F32)
        lw, kt, v, alpha, beta = (jnp.where(live, x, zero) for x in (lw, kt, v, alpha, beta))
    g_s[...] = g
    bo_s[...] = bonus

    ri = lax.broadcasted_iota(jnp.int32, (tt, tt), 0)
    rj = lax.broadcasted_iota(jnp.int32, (tt, tt), 1)
    same_chunk = (ri // CHUNK) == (rj // CHUNK)
    tril_blk = jnp.where(rj <= ri, jnp.where(same_chunk, 1.0, 0.0), 0.0).astype(BF16)
    hi, mid, lo = _split3(lw)
    cum = _dot(tril_blk, hi) + _dot(tril_blk, mid) + _dot(tril_blk, lo)
    seg = min(tt, CHUNK)
    tot = jnp.concatenate([jnp.broadcast_to(cum[e - 1:e, :], (seg, d)) for e in range(seg, tt + 1, seg)],
                          axis=0)
    dec_in_ = jnp.exp(cum)
    dec_inv = jnp.exp(-cum)
    dec_prev = jnp.exp(cum - lw)
    dec_tail = jnp.exp(tot - cum)
    dec_all = jnp.exp(tot)

    def put(ref, val):
        if tt < rows_s:
            val = jnp.concatenate([val, jnp.zeros((rows_s - tt, d), F32)], axis=0)
        ref[...] = val.astype(BF16)

    put(at_b, alpha * dec_prev)
    put(rt_b, r * dec_in_)
    put(bh_b, beta * dec_inv)
    put(kh_b, kt * dec_inv)
    put(be_b, beta * dec_tail)
    put(ke_b, kt * dec_tail)
    put(v_b, v)
    for c in range(n_chunks):
        dall_s[c:c + 1, :] = dec_all[c * CHUNK:c * CHUNK + 1, :]

    pt = lax.broadcasted_iota(jnp.int32, (CHUNK, LANES), 0)
    pl_ = lax.broadcasted_iota(jnp.int32, (CHUNK, LANES), 1)
    ps = pl_ % HEAD_N
    strict = ps < pt
    incl = ps <= pt
    lane_lo = pl_ < HEAD_N
    eye_pair = jnp.where(ps == pt, 1.0, 0.0)
    zc = jnp.zeros((CHUNK, LANES), F32)

    m_lo = jnp.where(lane_lo, 1.0, 0.0).astype(BF16)
    m_hi = jnp.where(lane_lo, 0.0, 1.0).astype(BF16)

    def bd(xb):
        return jnp.concatenate([xb * m_lo, xb * m_hi], axis=0)

    def blk(ref, c, p):
        return ref[c * CHUNK:(c + 1) * CHUNK, p * LANES:(p + 1) * LANES]

    def put_blk(ref, c, p, val):
        ref[c * CHUNK:(c + 1) * CHUNK, p * LANES:(p + 1) * LANES] = val

    items = [(c, p) for c in range(n_chunks) for p in range(n_pairs)]
    for g0 in range(0, len(items), group):
        its = items[g0:g0 + group]
        ar = [_dot_nt(jnp.concatenate([blk(at_b, c, p), blk(rt_b, c, p)], axis=0),
                      jnp.concatenate([bd(blk(bh_b, c, p)), bd(blk(kh_b, c, p))], axis=0))
              for c, p in its]
        l_ab = [jnp.where(strict, a[0:CHUNK, 0:LANES], zc) for a in ar]
        for (c, p), a in zip(its, ar):
            put_blk(arb_s, c, p, jnp.where(incl, a[CHUNK:2 * CHUNK, 0:LANES], zc).astype(BF16))
        lv_yv = [_dot(jnp.concatenate([jnp.where(strict, a[0:CHUNK, LANES:2 * LANES], zc),
                                       jnp.where(incl, a[CHUNK:2 * CHUNK, LANES:2 * LANES], zc)],
                                      axis=0).astype(BF16), bd(blk(v_b, c, p)))
                 for (c, p), a in zip(its, ar)]
        for (c, p), m in zip(its, lv_yv):
            put_blk(lv_s, c, p, m[0:CHUNK])
            put_blk(yv_s, c, p, m[CHUNK:2 * CHUNK])
        tm = [eye_pair + l for l in l_ab]
        if n_sq > 0:
            pw = [_dot(l.astype(BF16), bd(l.astype(BF16))) for l in l_ab]
        for i in range(n_sq):
            if i == n_sq - 1:
                tm = [t + _dot(q.astype(BF16), bd(t.astype(BF16))) for t, q in zip(tm, pw)]
            else:
                pr = [_dot(q.astype(BF16), jnp.concatenate([bd(q.astype(BF16)), bd(t.astype(BF16))], axis=1))
                      for t, q in zip(tm, pw)]
                pw = [x[:, 0:LANES] for x in pr]
                tm = [t + x[:, LANES:2 * LANES] for t, x in zip(tm, pr)]
        for (c, p), t in zip(its, tm):
            put_blk(tm_s, c, p, t.astype(BF16))

    pairs = range(n_pairs)
    for c in range(n_chunks):
        s_old = [sbd_ref[p] for p in pairs]
        x = [_dot_nt(jnp.concatenate([blk(at_b, c, p), blk(rt_b, c, p)], axis=0), s_old[p].astype(BF16))
             for p in pairs]
        ub = [_dot(blk(tm_s, c, p), bd((x[p][0:CHUNK] + blk(lv_s, c, p)).astype(BF16))).astype(BF16)
              for p in pairs]
        upd = [_dot_tn(jnp.concatenate([ub[p], blk(v_b, c, p)], axis=0),
                       jnp.concatenate([blk(be_b, c, p), blk(ke_b, c, p)], axis=0)) for p in pairs]
        for p in pairs:
            sbd_ref[p] = (s_old[p] * dall_s[c:c + 1, p * LANES:(p + 1) * LANES]
                          + jnp.where(same_head, upd[p], 0.0))
            put_blk(y_s, c, p, x[p][CHUNK:2 * CHUNK] + blk(yv_s, c, p) + _dot(blk(arb_s, c, p), bd(ub[p])))

    y_n = []
    for c0 in range(0, d, MXU_N):
        yb = y_s[0:tt, c0:c0 + MXU_N]
        yc = yb - _dot(yb.astype(BF16), ones_bd) * (1.0 / HEAD_N)
        var = _dot((yc * yc).astype(BF16), ones_bd) * (1.0 / HEAD_N)
        y_n.append(yc * lax.rsqrt(var + GN_EPS))
    y_n = jnp.concatenate(y_n, axis=1) * lng_ref[...] + lnb_ref[...]
    ob_ref[0] = _dot(((y_n + bo_s[...]) * g_s[...]).astype(BF16), wout_ref[...])

    @pl.when(ti == pl.num_programs(1) - 1)
    def _():
        for p in range(n_pairs):
            sout_ref[0, 2 * p] = sbd_ref[p, 0:HEAD_N, 0:HEAD_N]
            sout_ref[0, 2 * p + 1] = sbd_ref[p, HEAD_N:2 * HEAD_N, HEAD_N:2 * HEAD_N]


def _rwkv_branch(proj, prev0, s0, mu, dbase, wdec, ibase, wiclr, wgate, kkw, ka, rk, lng, lnb, wout,
                 *, tt, t_valid):
    b, t, _ = proj.shape
    d = wout.shape[0]
    n_lora = wgate.shape[0] * 2
    n_pairs = d // LANES
    rows_s = max(tt, CHUNK)
    n_eff = min(CHUNK, t_valid)
    n_sq = max(0, math.ceil(math.log2(n_eff)) - 1)
    kern = functools.partial(_rwkv_kernel, tt=tt, t_valid=t_valid, n_sq=n_sq, group=2 * n_pairs)
    const = lambda shape: pl.BlockSpec(shape, lambda i, j: (0,) * len(shape))
    col = lambda c: pl.BlockSpec((1, tt, d), lambda i, j: (i, j, c))
    big = lambda dt: pltpu.VMEM((rows_s, d), dt)
    return pl.pallas_call(
        kern,
        grid=(b, t // tt),
        in_specs=[col(4), col(5), col(6),
                  pl.BlockSpec((1, tt, n_lora), lambda i, j: (i, j, 7 * d // n_lora)),
                  pl.BlockSpec((1, 1, 3 * d + n_lora), lambda i, j: (i, 0, 0)),
                  pl.BlockSpec((1, 2 * n_pairs, HEAD_N, HEAD_N), lambda i, j: (i, 0, 0, 0)),
                  const((1, 3 * d + n_lora)), const((1, d)), const((LANES, d)), const((1, d)),
                  const((LANES, d)), const((LANES, d)), const((1, d)), const((1, d)), const((1, d)),
                  const((1, d)), const((1, d)), const((d, d))],
        out_specs=[pl.BlockSpec((1, tt, d), lambda i, j: (i, j, 0)),
                   pl.BlockSpec((1, 2 * n_pairs, HEAD_N, HEAD_N), lambda i, j: (i, 0, 0, 0))],
        out_shape=[jax.ShapeDtypeStruct((b, t, d), F32),
                   jax.ShapeDtypeStruct((b, 2 * n_pairs, HEAD_N, HEAD_N), F32)],
        scratch_shapes=[pltpu.VMEM((8 + tt, 3 * d + n_lora), F32),
                        big(BF16), big(BF16), big(BF16), big(BF16), big(BF16), big(BF16), big(BF16),
                        pltpu.VMEM((max(rows_s // CHUNK, 8), d), F32),
                        pltpu.VMEM((tt, d), F32), pltpu.VMEM((tt, d), F32),
                        big(BF16), big(BF16), big(F32), big(F32), big(F32),
                        pltpu.VMEM((n_pairs, LANES, LANES), F32)],
        compiler_params=_cparams(2),
        name="rwkv_branch",
    )(proj, proj, proj, proj, prev0, s0, mu, dbase, wdec, ibase, wiclr, wgate, kkw, ka, rk, lng, lnb, wout)


def _ffn_kernel(x_ref, ga_ref, gb_ref, oa_ref, ob_ref, wo_ref, g1_ref, g2_ref, wu_ref, wd_ref, g3_ref,
                o_ref, *, ff_chunk):
    merged = _sigmoid(ga_ref[...]) * oa_ref[...] + _sigmoid(gb_ref[...]) * ob_ref[...]
    m = _dot(merged.astype(BF16), wo_ref[...])
    x1 = x_ref[...] + _rms(m, g1_ref[...])
    h2 = _rms(x1, g2_ref[...]).astype(BF16)
    f = None
    for c0 in range(0, wu_ref.shape[1], ff_chunk):
        up = jnp.maximum(_dot(h2, wu_ref[:, c0:c0 + ff_chunk]), 0.0)
        part = _dot((up * up).astype(BF16), wd_ref[c0:c0 + ff_chunk, :])
        f = part if f is None else f + part
    o_ref[...] = x1 + _rms(f, g3_ref[...])


def _merge_ffn(x, proj, o_a, o_b, w_out, g1, g2, w_up, w_down, g3, *, tm):
    m, d = x.shape
    dff = w_up.shape[1]
    tok = lambda c: pl.BlockSpec((tm, d), lambda i: (i, c))
    const = lambda shape: pl.BlockSpec(shape, lambda i: (0, 0), pipeline_mode=pl.Buffered(1))
    return pl.pallas_call(
        functools.partial(_ffn_kernel, ff_chunk=d),
        grid=(m // tm,),
        in_specs=[tok(0), tok(2), tok(3), tok(0), tok(0),
                  const((d, d)), const((1, d)), const((1, d)), const((d, dff)), const((dff, d)),
                  const((1, d))],
        out_specs=tok(0),
        out_shape=jax.ShapeDtypeStruct((m, d), F32),
        compiler_params=_cparams(1),
        name="merge_ffn",
    )(x, proj, proj, o_a, o_b, w_out, g1, g2, w_up, w_down, g3)


def _layer(x, conv_state, prev0, s0, w, *, t_valid, tt, tm, tm_in, nb_conv):
    b, t, d = x.shape
    x2 = x.reshape(b * t, d)
    tm, tm_in = min(tm, b * t), min(tm_in, b * t)
    proj = _norm_matmul(x2, w["pre_mix_g"], w["w_in"], tm=tm_in, tn=w["w_in"].shape[1] // 2,
                        apply_norm=True, name="in_proj")
    proj3 = proj.reshape(b, t, -1)
    o_a, tail = _conv_branch(proj3, conv_state, w["conv_w"], w["conv_b"], w["conv_ln_g"], w["conv_ln_b"],
                             w["w_conv_out"], nb=nb_conv, tt=tt, t_valid=min(t_valid, tt))
    o_b, s_new = _rwkv_branch(proj3, prev0, s0, w["mu"], w["decay_base"], w["w_dec"], w["iclr_base"],
                              w["w_iclr"], w["w_gate"], w["k_k"], w["k_a"], w["r_k"], w["lnx_g"],
                              w["lnx_b"], w["w_rwkv_out"], tt=tt, t_valid=min(t_valid, tt))
    y = _merge_ffn(x2, proj, o_a.reshape(b * t, d), o_b.reshape(b * t, d), w["w_out"], w["post_mix_g"],
                   w["pre_ffn_g"], w["w_ff_up"], w["w_ff_down"], w["post_ffn_g"], tm=tm)
    return y.reshape(b, t, d), tail, s_new


def kernel(x_prompt, x_sample, state_conv, state_shift, state_wkv, pre_mix_g, post_mix_g, pre_ffn_g, post_ffn_g, w_in, conv_w, conv_b, conv_ln_g, conv_ln_b, w_conv_out, shift_mu, decay_base, w_decay_up, iclr_base, w_iclr_up, w_gate_up, k_k, k_a, r_k, lnx_g, lnx_b, w_rwkv_out, w_out, w_ff_up, w_ff_down):
    depth = w_in.shape[0]
    assert depth == 1
    d = x_prompt.shape[-1]
    n_heads = state_wkv.shape[2]
    r_dec, r_iclr, r_gate = w_decay_up.shape[1], w_iclr_up.shape[1], w_gate_up.shape[1]
    assert n_heads * HEAD_N == d and r_dec + r_iclr == LANES and r_gate == LANES
    n_lora = r_dec + r_iclr + r_gate
    i0 = 2 * d
    i1 = i0 + 3 * d + n_lora
    row = lambda a: a.reshape(1, -1)

    w_in_l = w_in[0]
    w_in_p = jnp.concatenate([w_in_l[:, :i0], w_in_l[:, i1:], w_in_l[:, i0:i1]], axis=1).astype(BF16)
    zpad = lambda n: jnp.zeros((n, d), F32)
    w = dict(
        pre_mix_g=row(pre_mix_g[0]), post_mix_g=row(post_mix_g[0]), pre_ffn_g=row(pre_ffn_g[0]),
        post_ffn_g=row(post_ffn_g[0]), w_in=w_in_p,
        conv_w=conv_w[0], conv_b=row(conv_b[0]), conv_ln_g=row(conv_ln_g[0]), conv_ln_b=row(conv_ln_b[0]),
        w_conv_out=w_conv_out[0].astype(BF16), mu=row(shift_mu[0]),
        decay_base=row(decay_base[0]), iclr_base=row(iclr_base[0]),
        w_dec=jnp.concatenate([w_decay_up[0], zpad(r_iclr)], axis=0).astype(BF16),
        w_iclr=jnp.concatenate([zpad(r_dec), w_iclr_up[0]], axis=0).astype(BF16),
        w_gate=w_gate_up[0].astype(BF16),
        k_k=row(k_k[0]), k_a=row(k_a[0]), r_k=row(r_k[0]), lnx_g=row(lnx_g[0]), lnx_b=row(lnx_b[0]),
        w_rwkv_out=w_rwkv_out[0].astype(BF16), w_out=w_out[0].astype(BF16),
        w_ff_up=w_ff_up[0].astype(BF16), w_ff_down=w_ff_down[0].astype(BF16),
    )

    bp, tp, _ = x_prompt.shape
    yp, conv_p, s_p = _layer(
        x_prompt, jnp.zeros((bp, conv_w.shape[1] - 1, d), F32), jnp.zeros((bp, 1, 3 * d + n_lora), F32),
        jnp.zeros((bp, n_heads, HEAD_N, HEAD_N), F32), w, t_valid=tp, tt=256, tm=256, tm_in=512, nb_conv=1)
    shift_p = _rms_rows(x_prompt[:, -1], w["pre_mix_g"])

    bs, ts, _ = x_sample.shape
    t_pad = 8
    xs = jnp.pad(x_sample, ((0, 0), (0, t_pad - ts), (0, 0)))
    prev0 = _norm_matmul(state_shift[0], w["pre_mix_g"], w_in_p[:, 4 * d:], tm=bs, tn=3 * d + n_lora,
                         apply_norm=False, name="shift_proj")
    ys, conv_s, s_s = _layer(
        xs, state_conv[0], prev0.reshape(bs, 1, -1), state_wkv[0], w,
        t_valid=ts, tt=t_pad, tm=256, tm_in=512, nb_conv=8)
    shift_s = _rms_rows(x_sample[:, -1], w["pre_mix_g"])

    return (yp, ys[:, :ts], conv_p[None], shift_p[None], s_p[None],
            conv_s[None], shift_s[None], s_s[None])
```

```python
import functools
import math

import jax
import jax.numpy as jnp
from jax import lax
from jax.experimental import pallas as pl
from jax.experimental.pallas import tpu as pltpu

F32 = jnp.float32
BF16 = jnp.bfloat16

RMS_EPS = 1e-6
LN_EPS = 1e-5
GN_EPS = 64e-5
HEAD_N = 64
LANES = 128
SUBLANES = 8
MXU_N = 256
CHUNK = 64
CONV_HALO = 32
VMEM_LIMIT = 52 * 1024 * 1024


def _cparams(n_axes):
    return pltpu.CompilerParams(dimension_semantics=("arbitrary",) * n_axes,
                                vmem_limit_bytes=VMEM_LIMIT)


def _rms(x, g):
    return x * lax.rsqrt(jnp.mean(x * x, axis=-1, keepdims=True) + RMS_EPS) * g


def _sigmoid(x):
    return 0.5 * jnp.tanh(0.5 * x) + 0.5


def _dot(a, b):
    return jnp.dot(a, b, preferred_element_type=F32)


def _dot_nt(a, b):
    return lax.dot_general(a, b, (((1,), (1,)), ((), ())), preferred_element_type=F32)


def _dot_tn(a, b):
    return lax.dot_general(a, b, (((0,), (0,)), ((), ())), preferred_element_type=F32)


def _proj_kernel(x_ref, g_ref, w_ref, o_ref, *, apply_norm):
    x = x_ref[...]
    if apply_norm:
        x = _rms(x, g_ref[...])
    o_ref[...] = _dot(x.astype(BF16), w_ref[...])


def _norm_matmul(x, g, w, *, tm, tn, apply_norm, name):
    m, d = x.shape
    n = w.shape[1]
    assert m % tm == 0 and n % tn == 0
    return pl.pallas_call(
        functools.partial(_proj_kernel, apply_norm=apply_norm),
        grid=(n // tn, m // tm),
        in_specs=[pl.BlockSpec((tm, d), lambda j, i: (i, 0)),
                  pl.BlockSpec((1, d), lambda j, i: (0, 0)),
                  pl.BlockSpec((d, tn), lambda j, i: (0, j))],
        out_specs=pl.BlockSpec((tm, tn), lambda j, i: (i, j)),
        out_shape=jax.ShapeDtypeStruct((m, n), F32),
        compiler_params=_cparams(2),
        name=name,
    )(x, g, w)


def _rms_rows_kernel(x_ref, g_ref, o_ref):
    o_ref[...] = _rms(x_ref[...], g_ref[...])


def _rms_rows(x, g):
    return pl.pallas_call(_rms_rows_kernel, out_shape=jax.ShapeDtypeStruct(x.shape, F32),
                          name="shift_rows")(x, g)


def _conv_kernel(val_ref, gate_ref, st_ref, cw_ref, cb_ref, lg_ref, lb_ref, wo_ref,
                 oa_ref, tail_ref, ext_ref, carry_ref, c_ref, wb_ref, *, nb, tt, t_valid, rb, n_taps):
    halo = n_taps - 1
    lead = CONV_HALO - halo
    rows = CONV_HALO + tt
    d = cw_ref.shape[1]

    @pl.when(pl.program_id(1) == 0)
    def _():
        carry_ref[...] = st_ref[...]

    @pl.when((pl.program_id(0) == 0) & (pl.program_id(1) == 0))
    def _():
        for j in range(n_taps):
            wb_ref[j * SUBLANES:(j + 1) * SUBLANES, :] = jnp.broadcast_to(cw_ref[j:j + 1, :], (SUBLANES, d))

    for i in range(nb):
        ext_ref[0, 0:lead, :] = jnp.zeros((lead, d), F32)
        ext_ref[0, lead:CONV_HALO, :] = carry_ref[i]
        ext_ref[0, CONV_HALO:rows, :] = val_ref[i] * _sigmoid(gate_ref[i])
        for s in range(1, SUBLANES):
            ext_ref[s, 0:rows - SUBLANES, :] = ext_ref[0, s:s + rows - SUBLANES, :]
        for r0 in range(0, tt, rb):
            acc = jnp.broadcast_to(cb_ref[...], (rb, d))
            for j in range(n_taps):
                q, s = divmod(lead + j, SUBLANES)
                a = q * SUBLANES + r0
                w8 = wb_ref[j * SUBLANES:(j + 1) * SUBLANES, :]
                acc = acc + ext_ref[s, a:a + rb, :] * jnp.concatenate([w8] * (rb // SUBLANES), axis=0)
            c_ref[i * tt + r0:i * tt + r0 + rb, :] = acc
        tail = ext_ref[0, lead + t_valid:lead + t_valid + halo, :]
        carry_ref[i] = tail
        tail_ref[i] = tail
    c = c_ref[...]
    mu = jnp.mean(c, axis=-1, keepdims=True)
    cc = c - mu
    var = jnp.mean(cc * cc, axis=-1, keepdims=True)
    z = cc * lax.rsqrt(var + LN_EPS) * lg_ref[...] + lb_ref[...]
    z = z * _sigmoid(z)
    out = _dot(z.astype(BF16), wo_ref[...])
    for i in range(nb):
        oa_ref[i] = out[i * tt:(i + 1) * tt]


def _conv_branch(proj, conv_state, conv_w, conv_b, ln_g, ln_b, w_conv_out, *, nb, tt, t_valid):
    b, t, _ = proj.shape
    d = conv_w.shape[1]
    n_taps = conv_w.shape[0]
    assert b % nb == 0 and t % tt == 0 and (nb == 1 or t == tt)
    rb = min(tt, 16)
    kern = functools.partial(_conv_kernel, nb=nb, tt=tt, t_valid=t_valid, rb=rb, n_taps=n_taps)
    row = lambda shape: pl.BlockSpec(shape, lambda i, j: (0, 0))
    return pl.pallas_call(
        kern,
        grid=(b // nb, t // tt),
        in_specs=[pl.BlockSpec((nb, tt, d), lambda i, j: (i, j, 0)),
                  pl.BlockSpec((nb, tt, d), lambda i, j: (i, j, 1)),
                  pl.BlockSpec((nb, n_taps - 1, d), lambda i, j: (i, 0, 0)),
                  row((n_taps, d)), row((1, d)), row((1, d)), row((1, d)), row((d, d))],
        out_specs=[pl.BlockSpec((nb, tt, d), lambda i, j: (i, j, 0)),
                   pl.BlockSpec((nb, n_taps - 1, d), lambda i, j: (i, 0, 0))],
        out_shape=[jax.ShapeDtypeStruct((b, t, d), F32),
                   jax.ShapeDtypeStruct((b, n_taps - 1, d), F32)],
        scratch_shapes=[pltpu.VMEM((SUBLANES, CONV_HALO + tt, d), F32),
                        pltpu.VMEM((nb, n_taps - 1, d), F32),
                        pltpu.VMEM((nb * tt, d), F32),
                        pltpu.VMEM((n_taps * SUBLANES, d), F32)],
        compiler_params=_cparams(2),
        name="conv_branch",
    )(proj, proj, conv_state, conv_w, conv_b, ln_g, ln_b, w_conv_out)


def _split3(x):
    hi = x.astype(BF16)
    r1 = x - hi.astype(F32)
    mid = r1.astype(BF16)
    lo = (r1 - mid.astype(F32)).astype(BF16)
    return hi, mid, lo


def _rwkv_kernel(r_ref, k_ref, v_ref, lo_ref, prev_ref, s0_ref,
                 mu_ref, dbase_ref, wdec_ref, ibase_ref, wiclr_ref, wgate_ref,
                 kkw_ref, ka_ref, rk_ref, lng_ref, lnb_ref, wout_ref,
                 ob_ref, sout_ref,
                 sh_ref, at_b, rt_b, bh_b, kh_b, be_b, ke_b, v_b, dall_s, g_s, bo_s,
                 tm_s, arb_s, lv_s, yv_s, y_s, sbd_ref,
                 *, nb, tt, t_valid, n_sq, group):
    d = r_ref.shape[2]
    n_lora = lo_ref.shape[2]
    n_pairs = d // LANES
    rows_t = nb * tt
    rows_s = at_b.shape[0]
    n_chunks = rows_s // CHUNK
    seg = min(tt, CHUNK)
    ti = pl.program_id(1)

    @pl.when(ti == 0)
    def _():
        zh = jnp.zeros((HEAD_N, HEAD_N), F32)
        for i in range(nb):
            for p in range(n_pairs):
                sbd_ref[i, p] = jnp.concatenate(
                    [jnp.concatenate([s0_ref[i, 2 * p], zh], axis=1),
                     jnp.concatenate([zh, s0_ref[i, 2 * p + 1]], axis=1)], axis=0)

    for i in range(nb):
        r0 = 8 + i * tt
        sh_ref[r0:r0 + tt, 0:d] = r_ref[i]
        sh_ref[r0:r0 + tt, d:2 * d] = k_ref[i]
        sh_ref[r0:r0 + tt, 2 * d:3 * d] = v_ref[i]
        sh_ref[r0:r0 + tt, 3 * d:3 * d + n_lora] = lo_ref[i]

    @pl.when(ti == 0)
    def _():
        for i in range(nb):
            sh_ref[7 + i * tt:8 + i * tt, :] = prev_ref[i]

    def mixed(c0, c1):
        cur = sh_ref[8:8 + rows_t, c0:c1]
        prv = sh_ref[7:7 + rows_t, c0:c1]
        return cur + (prv - cur) * mu_ref[:, c0:c1]

    r = mixed(0, d)
    k = mixed(d, 2 * d)
    v = mixed(2 * d, 3 * d)
    lora = mixed(3 * d, 3 * d + n_lora)
    if nb == 1:
        sh_ref[7:8, :] = sh_ref[7 + t_valid:8 + t_valid, :]

    l_di = lora[:, 0:LANES]
    l_g = lora[:, LANES:2 * LANES]
    dec_in = dbase_ref[...] + _dot(jnp.tanh(l_di).astype(BF16), wdec_ref[...])
    a = _sigmoid(ibase_ref[...] + _dot(l_di.astype(BF16), wiclr_ref[...]))
    g = _dot(_sigmoid(l_g).astype(BF16), wgate_ref[...])
    z = -dec_in
    softplus = jnp.maximum(z, 0.0) + jnp.log(1.0 + jnp.exp(-jnp.abs(z)))
    lw = -jnp.exp(-softplus - 0.5)

    li = lax.broadcasted_iota(jnp.int32, (LANES, LANES), 0) // HEAD_N
    lj = lax.broadcasted_iota(jnp.int32, (LANES, LANES), 1) // HEAD_N
    same_head = li == lj
    wi = lax.broadcasted_iota(jnp.int32, (MXU_N, MXU_N), 0) // HEAD_N
    wj = lax.broadcasted_iota(jnp.int32, (MXU_N, MXU_N), 1) // HEAD_N
    ones_bd = jnp.where(wi == wj, 1.0, 0.0).astype(BF16)

    def head_sum(x):
        parts = [_dot(x[:, c0:c0 + MXU_N].astype(BF16), ones_bd) for c0 in range(0, d, MXU_N)]
        return jnp.concatenate(parts, axis=1)

    kk = k * kkw_ref[...]
    kk = kk * lax.rsqrt(jnp.maximum(head_sum(kk * kk), 1e-24))
    kt = k * (1.0 + (a - 1.0) * ka_ref[...])
    bonus = head_sum(r * kt * rk_ref[...]) * v
    alpha = -kk
    beta = kk * a
    if t_valid < tt:
        live = lax.broadcasted_iota(jnp.int32, (rows_t, d), 0) % tt < t_valid
        zero = jnp.zeros((rows_t, d), F32)
        lw, kt, v, alpha, beta = (jnp.where(live, x, zero) for x in (lw, kt, v, alpha, beta))
    g_s[...] = g
    bo_s[...] = bonus

    ri = lax.broadcasted_iota(jnp.int32, (rows_t, rows_t), 0)
    rj = lax.broadcasted_iota(jnp.int32, (rows_t, rows_t), 1)
    same_seg = (ri // seg) == (rj // seg)
    tril_blk = jnp.where(rj <= ri, jnp.where(same_seg, 1.0, 0.0), 0.0).astype(BF16)
    hi, mid, lo = _split3(lw)
    cum = _dot(tril_blk, hi) + _dot(tril_blk, mid) + _dot(tril_blk, lo)
    tot = jnp.concatenate([jnp.broadcast_to(cum[e - 1:e, :], (seg, d)) for e in range(seg, rows_t + 1, seg)],
                          axis=0)
    dec_in_ = jnp.exp(cum)
    dec_inv = jnp.exp(-cum)
    dec_prev = jnp.exp(cum - lw)
    dec_tail = jnp.exp(tot - cum)
    dec_all = jnp.exp(tot)

    def put(ref, val):
        ref[...] = val.astype(ref.dtype)

    put(at_b, alpha * dec_prev)
    put(rt_b, r * dec_in_)
    put(bh_b, beta * dec_inv)
    put(kh_b, kt * dec_inv)
    put(be_b, beta * dec_tail)
    put(ke_b, kt * dec_tail)
    put(v_b, v)
    for sg in range(rows_t // seg):
        dall_s[sg:sg + 1, :] = dec_all[sg * seg:sg * seg + 1, :]

    pt = lax.broadcasted_iota(jnp.int32, (CHUNK, LANES), 0)
    pl_ = lax.broadcasted_iota(jnp.int32, (CHUNK, LANES), 1)
    ps = pl_ % HEAD_N
    strict = ps < pt
    incl = ps <= pt
    if seg < CHUNK:
        own = (ps // seg) == (pt // seg)
        strict = jnp.logical_and(own, strict)
        incl = jnp.logical_and(own, incl)
    lane_lo = pl_ < HEAD_N
    eye_pair = jnp.where(ps == pt, 1.0, 0.0)
    zc = jnp.zeros((CHUNK, LANES), F32)

    m_lo = jnp.where(lane_lo, 1.0, 0.0).astype(BF16)
    m_hi = jnp.where(lane_lo, 0.0, 1.0).astype(BF16)

    def bd(xb):
        return jnp.concatenate([xb * m_lo, xb * m_hi], axis=0)

    def blk(ref, c, p):
        return ref[c * CHUNK:(c + 1) * CHUNK, p * LANES:(p + 1) * LANES]

    def blkb(ref, c, p):
        return blk(ref, c, p).astype(BF16)

    def put_blk(ref, c, p, val):
        ref[c * CHUNK:(c + 1) * CHUNK, p * LANES:(p + 1) * LANES] = val

    items = [(c, p) for c in range(n_chunks) for p in range(n_pairs)]
    for g0 in range(0, len(items), group):
        its = items[g0:g0 + group]
        ar = [_dot_nt(jnp.concatenate([blkb(at_b, c, p), blkb(rt_b, c, p)], axis=0),
                      jnp.concatenate([bd(blkb(bh_b, c, p)), bd(blkb(kh_b, c, p))], axis=0))
              for c, p in its]
        l_ab = [jnp.where(strict, a[0:CHUNK, 0:LANES], zc) for a in ar]
        for (c, p), a in zip(its, ar):
            put_blk(arb_s, c, p, jnp.where(incl, a[CHUNK:2 * CHUNK, 0:LANES], zc).astype(BF16))
        lv_yv = [_dot(jnp.concatenate([jnp.where(strict, a[0:CHUNK, LANES:2 * LANES], zc),
                                       jnp.where(incl, a[CHUNK:2 * CHUNK, LANES:2 * LANES], zc)],
                                      axis=0).astype(BF16), bd(blkb(v_b, c, p)))
                 for (c, p), a in zip(its, ar)]
        for (c, p), m in zip(its, lv_yv):
            put_blk(lv_s, c, p, m[0:CHUNK])
            put_blk(yv_s, c, p, m[CHUNK:2 * CHUNK])
        tm = [eye_pair + l for l in l_ab]
        if n_sq > 0:
            pw = [_dot(l.astype(BF16), bd(l.astype(BF16))) for l in l_ab]
        for i in range(n_sq):
            if i == n_sq - 1:
                tm = [t + _dot(q.astype(BF16), bd(t.astype(BF16))) for t, q in zip(tm, pw)]
            else:
                pr = [_dot(q.astype(BF16), jnp.concatenate([bd(q.astype(BF16)), bd(t.astype(BF16))], axis=1))
                      for t, q in zip(tm, pw)]
                pw = [x[:, 0:LANES] for x in pr]
                tm = [t + x[:, LANES:2 * LANES] for t, x in zip(tm, pr)]
        for (c, p), t in zip(its, tm):
            put_blk(tm_s, c, p, t.astype(BF16))

    pairs = range(n_pairs)
    segs = range(CHUNK // seg)

    def cat_rows(parts):
        return parts[0] if len(parts) == 1 else jnp.concatenate(parts, axis=0)

    def stack_bf16(top, bot):
        if seg % 16 == 0:
            return jnp.concatenate([top.astype(BF16), bot.astype(BF16)], axis=0)
        return jnp.concatenate([top.astype(F32), bot.astype(F32)], axis=0).astype(BF16)

    for c in range(n_chunks):
        def seq_of(sg):
            return (c * CHUNK + sg * seg) // tt if nb > 1 else 0

        def seg_rows(ref, sg, p):
            r0 = c * CHUNK + sg * seg
            return ref[r0:r0 + seg, p * LANES:(p + 1) * LANES]

        s_old = [[sbd_ref[seq_of(sg), p] for p in pairs] for sg in segs]
        x = [[_dot_nt(stack_bf16(seg_rows(at_b, sg, p), seg_rows(rt_b, sg, p)), s_old[sg][p].astype(BF16))
              for p in pairs] for sg in segs]
        x_a = [cat_rows([x[sg][p][0:seg] for sg in segs]) for p in pairs]
        x_r = [cat_rows([x[sg][p][seg:2 * seg] for sg in segs]) for p in pairs]
        u = [_dot(blk(tm_s, c, p), bd((x_a[p] + blk(lv_s, c, p)).astype(BF16))) for p in pairs]
        ub = [t.astype(BF16) for t in u]
        for p in pairs:
            put_blk(y_s, c, p, x_r[p] + blk(yv_s, c, p) + _dot(blk(arb_s, c, p), bd(ub[p])))
        for sg in segs:
            upd = [_dot_tn(stack_bf16(u[p][sg * seg:(sg + 1) * seg], seg_rows(v_b, sg, p)),
                           stack_bf16(seg_rows(be_b, sg, p), seg_rows(ke_b, sg, p))) for p in pairs]
            gi = c * (CHUNK // seg) + sg
            for p in pairs:
                sbd_ref[seq_of(sg), p] = (s_old[sg][p] * dall_s[gi:gi + 1, p * LANES:(p + 1) * LANES]
                                          + jnp.where(same_head, upd[p], 0.0))

    y_n = []
    for c0 in range(0, d, MXU_N):
        yb = y_s[0:rows_t, c0:c0 + MXU_N]
        yc = yb - _dot(yb.astype(BF16), ones_bd) * (1.0 / HEAD_N)
        var = _dot((yc * yc).astype(BF16), ones_bd) * (1.0 / HEAD_N)
        y_n.append(yc * lax.rsqrt(var + GN_EPS))
    y_n = jnp.concatenate(y_n, axis=1) * lng_ref[...] + lnb_ref[...]
    out = _dot(((y_n + bo_s[...]) * g_s[...]).astype(BF16), wout_ref[...])
    for i in range(nb):
        ob_ref[i] = out[i * tt:(i + 1) * tt]

    @pl.when(ti == pl.num_programs(1) - 1)
    def _():
        for i in range(nb):
            for p in range(n_pairs):
                sout_ref[i, 2 * p] = sbd_ref[i, p, 0:HEAD_N, 0:HEAD_N]
                sout_ref[i, 2 * p + 1] = sbd_ref[i, p, HEAD_N:2 * HEAD_N, HEAD_N:2 * HEAD_N]


def _rwkv_branch(proj, prev0, s0, mu, dbase, wdec, ibase, wiclr, wgate, kkw, ka, rk, lng, lnb, wout,
                 *, nb, tt, t_valid):
    b, t, _ = proj.shape
    d = wout.shape[0]
    n_lora = wgate.shape[0] * 2
    n_pairs = d // LANES
    rows_s = nb * tt
    seg = min(tt, CHUNK)
    assert b % nb == 0 and t % tt == 0 and rows_s % CHUNK == 0 and max(tt, CHUNK) % seg == 0
    assert nb == 1 or (t == tt and t_valid < tt)
    n_sq = max(0, math.ceil(math.log2(min(seg, t_valid))) - 1)
    kern = functools.partial(_rwkv_kernel, nb=nb, tt=tt, t_valid=t_valid, n_sq=n_sq, group=2 * n_pairs)
    const = lambda shape: pl.BlockSpec(shape, lambda i, j: (0,) * len(shape))
    col = lambda c: pl.BlockSpec((nb, tt, d), lambda i, j: (i, j, c))
    big = lambda dt: pltpu.VMEM((rows_s, d), dt)
    op_dt = BF16 if seg % 16 == 0 else F32
    return pl.pallas_call(
        kern,
        grid=(b // nb, t // tt),
        in_specs=[col(4), col(5), col(6),
                  pl.BlockSpec((nb, tt, n_lora), lambda i, j: (i, j, 7 * d // n_lora)),
                  pl.BlockSpec((nb, 1, 3 * d + n_lora), lambda i, j: (i, 0, 0)),
                  pl.BlockSpec((nb, 2 * n_pairs, HEAD_N, HEAD_N), lambda i, j: (i, 0, 0, 0)),
                  const((1, 3 * d + n_lora)), const((1, d)), const((LANES, d)), const((1, d)),
                  const((LANES, d)), const((LANES, d)), const((1, d)), const((1, d)), const((1, d)),
                  const((1, d)), const((1, d)), const((d, d))],
        out_specs=[pl.BlockSpec((nb, tt, d), lambda i, j: (i, j, 0)),
                   pl.BlockSpec((nb, 2 * n_pairs, HEAD_N, HEAD_N), lambda i, j: (i, 0, 0, 0))],
        out_shape=[jax.ShapeDtypeStruct((b, t, d), F32),
                   jax.ShapeDtypeStruct((b, 2 * n_pairs, HEAD_N, HEAD_N), F32)],
        scratch_shapes=[pltpu.VMEM((8 + rows_s, 3 * d + n_lora), F32),
                        big(op_dt), big(op_dt), big(BF16), big(BF16), big(op_dt), big(op_dt), big(op_dt),
                        pltpu.VMEM((max(rows_s // seg, SUBLANES), d), F32),
                        big(F32), big(F32),
                        big(BF16), big(BF16), big(F32), big(F32), big(F32),
                        pltpu.VMEM((nb, n_pairs, LANES, LANES), F32)],
        compiler_params=_cparams(2),
        name="rwkv_branch",
    )(proj, proj, proj, proj, prev0, s0, mu, dbase, wdec, ibase, wiclr, wgate, kkw, ka, rk, lng, lnb, wout)


def _ffn_kernel(x_ref, ga_ref, gb_ref, oa_ref, ob_ref, wo_ref, g1_ref, g2_ref, wu_ref, wd_ref, g3_ref,
                o_ref, *, ff_chunk):
    merged = _sigmoid(ga_ref[...]) * oa_ref[...] + _sigmoid(gb_ref[...]) * ob_ref[...]
    m = _dot(merged.astype(BF16), wo_ref[...])
    x1 = x_ref[...] + _rms(m, g1_ref[...])
    h2 = _rms(x1, g2_ref[...]).astype(BF16)
    f = None
    for c0 in range(0, wu_ref.shape[1], ff_chunk):
        up = jnp.maximum(_dot(h2, wu_ref[:, c0:c0 + ff_chunk]), 0.0)
        part = _dot((up * up).astype(BF16), wd_ref[c0:c0 + ff_chunk, :])
        f = part if f is None else f + part
    o_ref[...] = x1 + _rms(f, g3_ref[...])


def _merge_ffn(x, proj, o_a, o_b, w_out, g1, g2, w_up, w_down, g3, *, tm):
    m, d = x.shape
    dff = w_up.shape[1]
    tok = lambda c: pl.BlockSpec((tm, d), lambda i: (i, c))
    const = lambda shape: pl.BlockSpec(shape, lambda i: (0, 0), pipeline_mode=pl.Buffered(1))
    return pl.pallas_call(
        functools.partial(_ffn_kernel, ff_chunk=d),
        grid=(m // tm,),
        in_specs=[tok(0), tok(2), tok(3), tok(0), tok(0),
                  const((d, d)), const((1, d)), const((1, d)), const((d, dff)), const((dff, d)),
                  const((1, d))],
        out_specs=tok(0),
        out_shape=jax.ShapeDtypeStruct((m, d), F32),
        compiler_params=_cparams(1),
        name="merge_ffn",
    )(x, proj, proj, o_a, o_b, w_out, g1, g2, w_up, w_down, g3)


def _layer(x, conv_state, prev0, s0, w, *, t_valid, tt, tm, tm_in, nb):
    b, t, d = x.shape
    x2 = x.reshape(b * t, d)
    tm, tm_in = min(tm, b * t), min(tm_in, b * t)
    proj = _norm_matmul(x2, w["pre_mix_g"], w["w_in"], tm=tm_in, tn=w["w_in"].shape[1] // 2,
                        apply_norm=True, name="in_proj")
    proj3 = proj.reshape(b, t, -1)
    o_a, tail = _conv_branch(proj3, conv_state, w["conv_w"], w["conv_b"], w["conv_ln_g"], w["conv_ln_b"],
                             w["w_conv_out"], nb=nb, tt=tt, t_valid=min(t_valid, tt))
    o_b, s_new = _rwkv_branch(proj3, prev0, s0, w["mu"], w["decay_base"], w["w_dec"], w["iclr_base"],
                              w["w_iclr"], w["w_gate"], w["k_k"], w["k_a"], w["r_k"], w["lnx_g"],
                              w["lnx_b"], w["w_rwkv_out"], nb=nb, tt=tt, t_valid=min(t_valid, tt))
    y = _merge_ffn(x2, proj, o_a.reshape(b * t, d), o_b.reshape(b * t, d), w["w_out"], w["post_mix_g"],
                   w["pre_ffn_g"], w["w_ff_up"], w["w_ff_down"], w["post_ffn_g"], tm=tm)
    return y.reshape(b, t, d), tail, s_new


def kernel(x_prompt, x_sample, state_conv, state_shift, state_wkv, pre_mix_g, post_mix_g, pre_ffn_g, post_ffn_g, w_in, conv_w, conv_b, conv_ln_g, conv_ln_b, w_conv_out, shift_mu, decay_base, w_decay_up, iclr_base, w_iclr_up, w_gate_up, k_k, k_a, r_k, lnx_g, lnx_b, w_rwkv_out, w_out, w_ff_up, w_ff_down):
    depth = w_in.shape[0]
    assert depth == 1
    d = x_prompt.shape[-1]
    n_heads = state_wkv.shape[2]
    r_dec, r_iclr, r_gate = w_decay_up.shape[1], w_iclr_up.shape[1], w_gate_up.shape[1]
    assert n_heads * HEAD_N == d and r_dec + r_iclr == LANES and r_gate == LANES
    n_lora = r_dec + r_iclr + r_gate
    i0 = 2 * d
    i1 = i0 + 3 * d + n_lora
    row = lambda a: a.reshape(1, -1)

    w_in_l = w_in[0]
    w_in_p = jnp.concatenate([w_in_l[:, :i0], w_in_l[:, i1:], w_in_l[:, i0:i1]], axis=1).astype(BF16)
    zpad = lambda n: jnp.zeros((n, d), F32)
    w = dict(
        pre_mix_g=row(pre_mix_g[0]), post_mix_g=row(post_mix_g[0]), pre_ffn_g=row(pre_ffn_g[0]),
        post_ffn_g=row(post_ffn_g[0]), w_in=w_in_p,
        conv_w=conv_w[0], conv_b=row(conv_b[0]), conv_ln_g=row(conv_ln_g[0]), conv_ln_b=row(conv_ln_b[0]),
        w_conv_out=w_conv_out[0].astype(BF16), mu=row(shift_mu[0]),
        decay_base=row(decay_base[0]), iclr_base=row(iclr_base[0]),
        w_dec=jnp.concatenate([w_decay_up[0], zpad(r_iclr)], axis=0).astype(BF16),
        w_iclr=jnp.concatenate([zpad(r_dec), w_iclr_up[0]], axis=0).astype(BF16),
        w_gate=w_gate_up[0].astype(BF16),
        k_k=row(k_k[0]), k_a=row(k_a[0]), r_k=row(r_k[0]), lnx_g=row(lnx_g[0]), lnx_b=row(lnx_b[0]),
        w_rwkv_out=w_rwkv_out[0].astype(BF16), w_out=w_out[0].astype(BF16),
        w_ff_up=w_ff_up[0].astype(BF16), w_ff_down=w_ff_down[0].astype(BF16),
    )

    bp, tp, _ = x_prompt.shape
    yp, conv_p, s_p = _layer(
        x_prompt, jnp.zeros((bp, conv_w.shape[1] - 1, d), F32), jnp.zeros((bp, 1, 3 * d + n_lora), F32),
        jnp.zeros((bp, n_heads, HEAD_N, HEAD_N), F32), w, t_valid=tp, tt=256, tm=256, tm_in=512, nb=1)
    shift_p = _rms_rows(x_prompt[:, -1], w["pre_mix_g"])

    bs, ts, _ = x_sample.shape
    t_pad = 8
    xs = jnp.pad(x_sample, ((0, 0), (0, t_pad - ts), (0, 0)))
    prev0 = _norm_matmul(state_shift[0], w["pre_mix_g"], w_in_p[:, 4 * d:], tm=bs, tn=3 * d + n_lora,
                         apply_norm=False, name="shift_proj")
    ys, conv_s, s_s = _layer(
        xs, state_conv[0], prev0.reshape(bs, 1, -1), state_wkv[0], w,
        t_valid=ts, tt=t_pad, tm=256, tm_in=512, nb=8)
    shift_s = _rms_rows(x_sample[:, -1], w["pre_mix_g"])

    return (yp, ys[:, :ts], conv_p[None], shift_p[None], s_p[None],
            conv_s[None], shift_s[None], s_s[None])
```

```python
import functools
import math

import jax
import jax.numpy as jnp
from jax import lax
from jax.experimental import pallas as pl
from jax.experimental.pallas import tpu as pltpu

F32 = jnp.float32
BF16 = jnp.bfloat16

RMS_EPS = 1e-6
LN_EPS = 1e-5
GN_EPS = 64e-5
HEAD_N = 64
LANES = 128
SUBLANES = 8
MXU_N = 256
CHUNK = 64
CONV_HALO = 32
VMEM_LIMIT = 52 * 1024 * 1024


def _cparams(n_axes):
    return pltpu.CompilerParams(dimension_semantics=("arbitrary",) * n_axes,
                                vmem_limit_bytes=VMEM_LIMIT)


def _rms(x, g):
    return x * lax.rsqrt(jnp.mean(x * x, axis=-1, keepdims=True) + RMS_EPS) * g


def _sigmoid(x):
    return 0.5 * jnp.tanh(0.5 * x) + 0.5


def _dot(a, b):
    return jnp.dot(a, b, preferred_element_type=F32)


def _dot_nt(a, b):
    return lax.dot_general(a, b, (((1,), (1,)), ((), ())), preferred_element_type=F32)


def _dot_tn(a, b):
    return lax.dot_general(a, b, (((0,), (0,)), ((), ())), preferred_element_type=F32)


def _proj_kernel(x_ref, g_ref, w_ref, o_ref, *, apply_norm):
    x = x_ref[...]
    if apply_norm:
        x = _rms(x, g_ref[...])
    o_ref[...] = _dot(x.astype(BF16), w_ref[...])


def _norm_matmul(x, g, w, *, tm, tn, apply_norm, name):
    m, d = x.shape
    n = w.shape[1]
    assert m % tm == 0 and n % tn == 0
    return pl.pallas_call(
        functools.partial(_proj_kernel, apply_norm=apply_norm),
        grid=(n // tn, m // tm),
        in_specs=[pl.BlockSpec((tm, d), lambda j, i: (i, 0)),
                  pl.BlockSpec((1, d), lambda j, i: (0, 0)),
                  pl.BlockSpec((d, tn), lambda j, i: (0, j))],
        out_specs=pl.BlockSpec((tm, tn), lambda j, i: (i, j)),
        out_shape=jax.ShapeDtypeStruct((m, n), F32),
        compiler_params=_cparams(2),
        name=name,
    )(x, g, w)


def _rms_rows_kernel(x_ref, g_ref, o_ref):
    o_ref[...] = _rms(x_ref[...], g_ref[...])


def _rms_rows(x, g):
    return pl.pallas_call(_rms_rows_kernel, out_shape=jax.ShapeDtypeStruct(x.shape, F32),
                          name="shift_rows")(x, g)


def _conv_kernel(val_ref, gate_ref, st_ref, cw_ref, cb_ref, lg_ref, lb_ref, wo_ref,
                 oa_ref, tail_ref, ext_ref, carry_ref, c_ref, wb_ref, *, nb, tt, t_valid, rb, n_taps):
    halo = n_taps - 1
    lead = CONV_HALO - halo
    rows = CONV_HALO + tt
    d = cw_ref.shape[1]

    @pl.when(pl.program_id(1) == 0)
    def _():
        carry_ref[...] = st_ref[...]

    @pl.when((pl.program_id(0) == 0) & (pl.program_id(1) == 0))
    def _():
        for j in range(n_taps):
            wb_ref[j * SUBLANES:(j + 1) * SUBLANES, :] = jnp.broadcast_to(cw_ref[j:j + 1, :], (SUBLANES, d))

    for i in range(nb):
        ext_ref[0, 0:lead, :] = jnp.zeros((lead, d), F32)
        ext_ref[0, lead:CONV_HALO, :] = carry_ref[i]
        ext_ref[0, CONV_HALO:rows, :] = val_ref[i] * _sigmoid(gate_ref[i])
        for s in range(1, SUBLANES):
            ext_ref[s, 0:rows - SUBLANES, :] = ext_ref[0, s:s + rows - SUBLANES, :]
        for r0 in range(0, tt, rb):
            acc = jnp.broadcast_to(cb_ref[...], (rb, d))
            for j in range(n_taps):
                q, s = divmod(lead + j, SUBLANES)
                a = q * SUBLANES + r0
                w8 = wb_ref[j * SUBLANES:(j + 1) * SUBLANES, :]
                acc = acc + ext_ref[s, a:a + rb, :] * jnp.concatenate([w8] * (rb // SUBLANES), axis=0)
            c_ref[i * tt + r0:i * tt + r0 + rb, :] = acc
        tail = ext_ref[0, lead + t_valid:lead + t_valid + halo, :]
        carry_ref[i] = tail
        tail_ref[i] = tail
    c = c_ref[...]
    mu = jnp.mean(c, axis=-1, keepdims=True)
    cc = c - mu
    var = jnp.mean(cc * cc, axis=-1, keepdims=True)
    z = cc * lax.rsqrt(var + LN_EPS) * lg_ref[...] + lb_ref[...]
    z = z * _sigmoid(z)
    out = _dot(z.astype(BF16), wo_ref[...])
    for i in range(nb):
        oa_ref[i] = out[i * tt:(i + 1) * tt]


def _conv_branch(proj, conv_state, conv_w, conv_b, ln_g, ln_b, w_conv_out, *, nb, tt, t_valid):
    b, t, _ = proj.shape
    d = conv_w.shape[1]
    n_taps = conv_w.shape[0]
    assert b % nb == 0 and t % tt == 0 and (nb == 1 or t == tt)
    rb = min(tt, 16)
    kern = functools.partial(_conv_kernel, nb=nb, tt=tt, t_valid=t_valid, rb=rb, n_taps=n_taps)
    row = lambda shape: pl.BlockSpec(shape, lambda i, j: (0, 0))
    return pl.pallas_call(
        kern,
        grid=(b // nb, t // tt),
        in_specs=[pl.BlockSpec((nb, tt, d), lambda i, j: (i, j, 0)),
                  pl.BlockSpec((nb, tt, d), lambda i, j: (i, j, 1)),
                  pl.BlockSpec((nb, n_taps - 1, d), lambda i, j: (i, 0, 0)),
                  row((n_taps, d)), row((1, d)), row((1, d)), row((1, d)), row((d, d))],
        out_specs=[pl.BlockSpec((nb, tt, d), lambda i, j: (i, j, 0)),
                   pl.BlockSpec((nb, n_taps - 1, d), lambda i, j: (i, 0, 0))],
        out_shape=[jax.ShapeDtypeStruct((b, t, d), F32),
                   jax.ShapeDtypeStruct((b, n_taps - 1, d), F32)],
        scratch_shapes=[pltpu.VMEM((SUBLANES, CONV_HALO + tt, d), F32),
                        pltpu.VMEM((nb, n_taps - 1, d), F32),
                        pltpu.VMEM((nb * tt, d), F32),
                        pltpu.VMEM((n_taps * SUBLANES, d), F32)],
        compiler_params=_cparams(2),
        name="conv_branch",
    )(proj, proj, conv_state, conv_w, conv_b, ln_g, ln_b, w_conv_out)


def _split3(x):
    hi = x.astype(BF16)
    r1 = x - hi.astype(F32)
    mid = r1.astype(BF16)
    lo = (r1 - mid.astype(F32)).astype(BF16)
    return hi, mid, lo


def _rwkv_kernel(r_ref, k_ref, v_ref, lo_ref, prev_ref, s0_ref,
                 mu_ref, dbase_ref, wdec_ref, ibase_ref, wiclr_ref, wgate_ref,
                 kkw_ref, ka_ref, rk_ref, lng_ref, lnb_ref, wout_ref,
                 ob_ref, sout_ref,
                 sh_ref, at_b, rt_b, bh_b, kh_b, be_b, ke_b, v_b, dall_s, g_s, bo_s,
                 tm_s, arb_s, lv_s, yv_s, y_s, sbd_ref,
                 *, nb, tt, t_valid, n_sq):
    d = r_ref.shape[2]
    n_lora = lo_ref.shape[2]
    n_pairs = d // LANES
    rows_t = nb * tt
    rows_s = at_b.shape[0]
    n_chunks = rows_s // CHUNK
    seg = min(tt, CHUNK)
    sub = min(rows_t, 2 * CHUNK)
    ti = pl.program_id(1)

    @pl.when(ti == 0)
    def _():
        zh = jnp.zeros((HEAD_N, HEAD_N), F32)
        for i in range(nb):
            for p in range(n_pairs):
                sbd_ref[i, p] = jnp.concatenate(
                    [jnp.concatenate([s0_ref[i, 2 * p], zh], axis=1),
                     jnp.concatenate([zh, s0_ref[i, 2 * p + 1]], axis=1)], axis=0)

    for i in range(nb):
        r0 = 8 + i * tt
        sh_ref[r0:r0 + tt, 0:d] = r_ref[i]
        sh_ref[r0:r0 + tt, d:2 * d] = k_ref[i]
        sh_ref[r0:r0 + tt, 2 * d:3 * d] = v_ref[i]
        sh_ref[r0:r0 + tt, 3 * d:3 * d + n_lora] = lo_ref[i]

    @pl.when(ti == 0)
    def _():
        for i in range(nb):
            sh_ref[7 + i * tt:8 + i * tt, :] = prev_ref[i]

    li = lax.broadcasted_iota(jnp.int32, (LANES, LANES), 0) // HEAD_N
    lj = lax.broadcasted_iota(jnp.int32, (LANES, LANES), 1) // HEAD_N
    same_head = li == lj
    wi = lax.broadcasted_iota(jnp.int32, (MXU_N, MXU_N), 0) // HEAD_N
    wj = lax.broadcasted_iota(jnp.int32, (MXU_N, MXU_N), 1) // HEAD_N
    ones_bd = jnp.where(wi == wj, 1.0, 0.0).astype(BF16)

    def head_sum(x):
        parts = [_dot(x[:, c0:c0 + MXU_N].astype(BF16), ones_bd) for c0 in range(0, d, MXU_N)]
        return jnp.concatenate(parts, axis=1)

    ri = lax.broadcasted_iota(jnp.int32, (sub, sub), 0)
    rj = lax.broadcasted_iota(jnp.int32, (sub, sub), 1)
    same_seg = (ri // seg) == (rj // seg)
    tril_blk = jnp.where(rj <= ri, jnp.where(same_seg, 1.0, 0.0), 0.0).astype(BF16)

    def prep(q0):
        def mixed(c0, c1):
            cur = sh_ref[8 + q0:8 + q0 + sub, c0:c1]
            prv = sh_ref[7 + q0:7 + q0 + sub, c0:c1]
            return cur + (prv - cur) * mu_ref[:, c0:c1]

        lora = mixed(3 * d, 3 * d + n_lora)
        l_di = lora[:, 0:LANES]
        l_g = lora[:, LANES:2 * LANES]
        dec_in = dbase_ref[...] + _dot(jnp.tanh(l_di).astype(BF16), wdec_ref[...])
        a = _sigmoid(ibase_ref[...] + _dot(l_di.astype(BF16), wiclr_ref[...]))
        g_s[q0:q0 + sub, :] = _dot(_sigmoid(l_g).astype(BF16), wgate_ref[...])
        yield
        z = -dec_in
        softplus = jnp.maximum(z, 0.0) + jnp.log(1.0 + jnp.exp(-jnp.abs(z)))
        lw = -jnp.exp(-softplus - 0.5)
        live = None
        if t_valid < tt:
            live = (lax.broadcasted_iota(jnp.int32, (sub, d), 0) + q0) % tt < t_valid
            lw = jnp.where(live, lw, 0.0)
        hi, mid, lo = _split3(lw)
        cum = _dot(tril_blk, hi) + _dot(tril_blk, mid) + _dot(tril_blk, lo)
        tot = jnp.concatenate([jnp.broadcast_to(cum[e - 1:e, :], (seg, d)) for e in range(seg, sub + 1, seg)],
                              axis=0)
        dec_all = jnp.exp(tot)
        for sg in range(sub // seg):
            dall_s[q0 // seg + sg:q0 // seg + sg + 1, :] = dec_all[sg * seg:sg * seg + 1, :]
        yield

        def put(ref, val, mask=True):
            if mask and live is not None:
                val = jnp.where(live, val, 0.0)
            ref[q0:q0 + sub, :] = val.astype(ref.dtype)

        k = mixed(d, 2 * d)
        kk = k * kkw_ref[...]
        kk = kk * lax.rsqrt(jnp.maximum(head_sum(kk * kk), 1e-24))
        yield
        put(at_b, -kk * jnp.exp(cum - lw))
        beta = kk * a
        put(bh_b, beta * jnp.exp(-cum))
        yield
        put(be_b, beta * jnp.exp(tot - cum))
        kt = k * (1.0 + (a - 1.0) * ka_ref[...])
        yield
        put(kh_b, kt * jnp.exp(-cum))
        put(ke_b, kt * jnp.exp(tot - cum))
        yield
        r = mixed(0, d)
        v = mixed(2 * d, 3 * d)
        bo_s[q0:q0 + sub, :] = head_sum(r * kt * rk_ref[...]) * v
        yield
        put(rt_b, r * jnp.exp(cum), mask=False)
        put(v_b, v)
        yield

    pt = lax.broadcasted_iota(jnp.int32, (CHUNK, LANES), 0)
    pl_ = lax.broadcasted_iota(jnp.int32, (CHUNK, LANES), 1)
    ps = pl_ % HEAD_N
    strict = ps < pt
    incl = ps <= pt
    if seg < CHUNK:
        own = (ps // seg) == (pt // seg)
        strict = jnp.logical_and(own, strict)
        incl = jnp.logical_and(own, incl)
    lane_lo = pl_ < HEAD_N
    eye_pair = jnp.where(ps == pt, 1.0, 0.0)
    zc = jnp.zeros((CHUNK, LANES), F32)

    m_lo = jnp.where(lane_lo, 1.0, 0.0).astype(BF16)
    m_hi = jnp.where(lane_lo, 0.0, 1.0).astype(BF16)

    def bd(xb):
        return jnp.concatenate([xb * m_lo, xb * m_hi], axis=0)

    def blk(ref, c, p):
        return ref[c * CHUNK:(c + 1) * CHUNK, p * LANES:(p + 1) * LANES]

    def blkb(ref, c, p):
        return blk(ref, c, p).astype(BF16)

    def put_blk(ref, c, p, val):
        ref[c * CHUNK:(c + 1) * CHUNK, p * LANES:(p + 1) * LANES] = val

    def chunk_setup(its):
        ar = [_dot_nt(jnp.concatenate([blkb(at_b, c, p), blkb(rt_b, c, p)], axis=0),
                      jnp.concatenate([bd(blkb(bh_b, c, p)), bd(blkb(kh_b, c, p))], axis=0))
              for c, p in its]
        yield
        l_ab = [jnp.where(strict, a[0:CHUNK, 0:LANES], zc) for a in ar]
        for (c, p), a in zip(its, ar):
            put_blk(arb_s, c, p, jnp.where(incl, a[CHUNK:2 * CHUNK, 0:LANES], zc).astype(BF16))
        lv_yv = [_dot(jnp.concatenate([jnp.where(strict, a[0:CHUNK, LANES:2 * LANES], zc),
                                       jnp.where(incl, a[CHUNK:2 * CHUNK, LANES:2 * LANES], zc)],
                                      axis=0).astype(BF16), bd(blkb(v_b, c, p)))
                 for (c, p), a in zip(its, ar)]
        yield
        for (c, p), m in zip(its, lv_yv):
            put_blk(lv_s, c, p, m[0:CHUNK])
            put_blk(yv_s, c, p, m[CHUNK:2 * CHUNK])
        tm = [eye_pair + l for l in l_ab]
        if n_sq > 0:
            pw = [_dot(l.astype(BF16), bd(l.astype(BF16))) for l in l_ab]
            yield
        for i in range(n_sq):
            if i == n_sq - 1:
                tm = [t + _dot(q.astype(BF16), bd(t.astype(BF16))) for t, q in zip(tm, pw)]
            else:
                pr = [_dot(q.astype(BF16), jnp.concatenate([bd(q.astype(BF16)), bd(t.astype(BF16))], axis=1))
                      for t, q in zip(tm, pw)]
                pw = [x[:, 0:LANES] for x in pr]
                tm = [t + x[:, LANES:2 * LANES] for t, x in zip(tm, pr)]
            yield
        for (c, p), t in zip(its, tm):
            put_blk(tm_s, c, p, t.astype(BF16))
        yield

    pairs = range(n_pairs)
    segs = range(CHUNK // seg)

    def cat_rows(parts):
        return parts[0] if len(parts) == 1 else jnp.concatenate(parts, axis=0)

    def stack_bf16(top, bot):
        if seg % 16 == 0:
            return jnp.concatenate([top.astype(BF16), bot.astype(BF16)], axis=0)
        return jnp.concatenate([top.astype(F32), bot.astype(F32)], axis=0).astype(BF16)

    def chunk_step(c):
        def seq_of(sg):
            return (c * CHUNK + sg * seg) // tt if nb > 1 else 0

        def seg_rows(ref, sg, p):
            r0 = c * CHUNK + sg * seg
            return ref[r0:r0 + seg, p * LANES:(p + 1) * LANES]

        s_old = [[sbd_ref[seq_of(sg), p] for p in pairs] for sg in segs]
        x = [[_dot_nt(stack_bf16(seg_rows(at_b, sg, p), seg_rows(rt_b, sg, p)), s_old[sg][p].astype(BF16))
              for p in pairs] for sg in segs]
        yield
        x_a = [cat_rows([x[sg][p][0:seg] for sg in segs]) for p in pairs]
        x_r = [cat_rows([x[sg][p][seg:2 * seg] for sg in segs]) for p in pairs]
        u = [_dot(blk(tm_s, c, p), bd((x_a[p] + blk(lv_s, c, p)).astype(BF16))) for p in pairs]
        yield
        ub = [t.astype(BF16) for t in u]
        for p in pairs:
            put_blk(y_s, c, p, x_r[p] + blk(yv_s, c, p) + _dot(blk(arb_s, c, p), bd(ub[p])))
        yield
        for sg in segs:
            upd = [_dot_tn(stack_bf16(u[p][sg * seg:(sg + 1) * seg], seg_rows(v_b, sg, p)),
                           stack_bf16(seg_rows(be_b, sg, p), seg_rows(ke_b, sg, p))) for p in pairs]
            gi = c * (CHUNK // seg) + sg
            for p in pairs:
                sbd_ref[seq_of(sg), p] = (s_old[sg][p] * dall_s[gi:gi + 1, p * LANES:(p + 1) * LANES]
                                          + jnp.where(same_head, upd[p], 0.0))
            yield

    def finish(q0):
        y_n = []
        for c0 in range(0, d, MXU_N):
            yb = y_s[q0:q0 + sub, c0:c0 + MXU_N]
            yc = yb - _dot(yb.astype(BF16), ones_bd) * (1.0 / HEAD_N)
            var = _dot((yc * yc).astype(BF16), ones_bd) * (1.0 / HEAD_N)
            y_n.append(yc * lax.rsqrt(var + GN_EPS))
            yield
        y_n = jnp.concatenate(y_n, axis=1) * lng_ref[...] + lnb_ref[...]
        out = _dot(((y_n + bo_s[q0:q0 + sub, :]) * g_s[q0:q0 + sub, :]).astype(BF16), wout_ref[...])
        if nb == 1:
            ob_ref[0, q0:q0 + sub, :] = out
        else:
            for i in range(sub // tt):
                ob_ref[q0 // tt + i] = out[i * tt:(i + 1) * tt]
        yield

    cps = sub // CHUNK
    n_sub = rows_t // sub

    def chunk_walk(s):
        for c in range(cps):
            yield from chunk_step(s * cps + c)

    def round_robin(gens):
        while gens:
            for gen in list(gens):
                if next(gen, StopIteration) is StopIteration:
                    gens.remove(gen)

    for slot in range(n_sub + 3):
        live_gens = []
        if slot < n_sub:
            live_gens.append(prep(slot * sub))
        if 0 <= slot - 1 < n_sub:
            live_gens.append(chunk_setup([((slot - 1) * cps + c, p) for c in range(cps) for p in pairs]))
        if 0 <= slot - 2 < n_sub:
            live_gens.append(chunk_walk(slot - 2))
        if 0 <= slot - 3 < n_sub:
            live_gens.append(finish((slot - 3) * sub))
        round_robin(live_gens)
        if slot == 0 and nb == 1:
            sh_ref[7:8, :] = sh_ref[7 + t_valid:8 + t_valid, :]

    @pl.when(ti == pl.num_programs(1) - 1)
    def _():
        for i in range(nb):
            for p in range(n_pairs):
                sout_ref[i, 2 * p] = sbd_ref[i, p, 0:HEAD_N, 0:HEAD_N]
                sout_ref[i, 2 * p + 1] = sbd_ref[i, p, HEAD_N:2 * HEAD_N, HEAD_N:2 * HEAD_N]


def _rwkv_branch(proj, prev0, s0, mu, dbase, wdec, ibase, wiclr, wgate, kkw, ka, rk, lng, lnb, wout,
                 *, nb, tt, t_valid):
    b, t, _ = proj.shape
    d = wout.shape[0]
    n_lora = wgate.shape[0] * 2
    n_pairs = d // LANES
    rows_s = nb * tt
    seg = min(tt, CHUNK)
    assert b % nb == 0 and t % tt == 0 and rows_s % CHUNK == 0 and max(tt, CHUNK) % seg == 0
    assert nb == 1 or (t == tt and t_valid < tt)
    n_sq = max(0, math.ceil(math.log2(min(seg, t_valid))) - 1)
    kern = functools.partial(_rwkv_kernel, nb=nb, tt=tt, t_valid=t_valid, n_sq=n_sq)
    const = lambda shape: pl.BlockSpec(shape, lambda i, j: (0,) * len(shape))
    col = lambda c: pl.BlockSpec((nb, tt, d), lambda i, j: (i, j, c))
    big = lambda dt: pltpu.VMEM((rows_s, d), dt)
    op_dt = BF16 if seg % 16 == 0 else F32
    return pl.pallas_call(
        kern,
        grid=(b // nb, t // tt),
        in_specs=[col(4), col(5), col(6),
                  pl.BlockSpec((nb, tt, n_lora), lambda i, j: (i, j, 7 * d // n_lora)),
                  pl.BlockSpec((nb, 1, 3 * d + n_lora), lambda i, j: (i, 0, 0)),
                  pl.BlockSpec((nb, 2 * n_pairs, HEAD_N, HEAD_N), lambda i, j: (i, 0, 0, 0)),
                  const((1, 3 * d + n_lora)), const((1, d)), const((LANES, d)), const((1, d)),
                  const((LANES, d)), const((LANES, d)), const((1, d)), const((1, d)), const((1, d)),
                  const((1, d)), const((1, d)), const((d, d))],
        out_specs=[pl.BlockSpec((nb, tt, d), lambda i, j: (i, j, 0)),
                   pl.BlockSpec((nb, 2 * n_pairs, HEAD_N, HEAD_N), lambda i, j: (i, 0, 0, 0))],
        out_shape=[jax.ShapeDtypeStruct((b, t, d), F32),
                   jax.ShapeDtypeStruct((b, 2 * n_pairs, HEAD_N, HEAD_N), F32)],
        scratch_shapes=[pltpu.VMEM((8 + rows_s, 3 * d + n_lora), F32),
                        big(op_dt), big(op_dt), big(BF16), big(BF16), big(op_dt), big(op_dt), big(op_dt),
                        pltpu.VMEM((max(rows_s // seg, SUBLANES), d), F32),
                        big(F32), big(F32),
                        big(BF16), big(BF16), big(F32), big(F32), big(F32),
                        pltpu.VMEM((nb, n_pairs, LANES, LANES), F32)],
        compiler_params=_cparams(2),
        name="rwkv_branch",
    )(proj, proj, proj, proj, prev0, s0, mu, dbase, wdec, ibase, wiclr, wgate, kkw, ka, rk, lng, lnb, wout)


def _ffn_kernel(x_ref, ga_ref, gb_ref, oa_ref, ob_ref, wo_ref, g1_ref, g2_ref, wu_ref, wd_ref, g3_ref,
                o_ref, *, ff_chunk):
    merged = _sigmoid(ga_ref[...]) * oa_ref[...] + _sigmoid(gb_ref[...]) * ob_ref[...]
    m = _dot(merged.astype(BF16), wo_ref[...])
    x1 = x_ref[...] + _rms(m, g1_ref[...])
    h2 = _rms(x1, g2_ref[...]).astype(BF16)
    f = None
    for c0 in range(0, wu_ref.shape[1], ff_chunk):
        up = jnp.maximum(_dot(h2, wu_ref[:, c0:c0 + ff_chunk]), 0.0)
        part = _dot((up * up).astype(BF16), wd_ref[c0:c0 + ff_chunk, :])
        f = part if f is None else f + part
    o_ref[...] = x1 + _rms(f, g3_ref[...])


def _merge_ffn(x, proj, o_a, o_b, w_out, g1, g2, w_up, w_down, g3, *, tm):
    m, d = x.shape
    dff = w_up.shape[1]
    tok = lambda c: pl.BlockSpec((tm, d), lambda i: (i, c))
    const = lambda shape: pl.BlockSpec(shape, lambda i: (0, 0), pipeline_mode=pl.Buffered(1))
    return pl.pallas_call(
        functools.partial(_ffn_kernel, ff_chunk=d),
        grid=(m // tm,),
        in_specs=[tok(0), tok(2), tok(3), tok(0), tok(0),
                  const((d, d)), const((1, d)), const((1, d)), const((d, dff)), const((dff, d)),
                  const((1, d))],
        out_specs=tok(0),
        out_shape=jax.ShapeDtypeStruct((m, d), F32),
        compiler_params=_cparams(1),
        name="merge_ffn",
    )(x, proj, proj, o_a, o_b, w_out, g1, g2, w_up, w_down, g3)


def _layer(x, conv_state, prev0, s0, w, *, t_valid, tt, tt_rwkv, tm, tm_in, nb):
    b, t, d = x.shape
    x2 = x.reshape(b * t, d)
    tm, tm_in = min(tm, b * t), min(tm_in, b * t)
    proj = _norm_matmul(x2, w["pre_mix_g"], w["w_in"], tm=tm_in, tn=w["w_in"].shape[1] // 2,
                        apply_norm=True, name="in_proj")
    proj3 = proj.reshape(b, t, -1)
    o_a, tail = _conv_branch(proj3, conv_state, w["conv_w"], w["conv_b"], w["conv_ln_g"], w["conv_ln_b"],
                             w["w_conv_out"], nb=nb, tt=tt, t_valid=min(t_valid, tt))
    o_b, s_new = _rwkv_branch(proj3, prev0, s0, w["mu"], w["decay_base"], w["w_dec"], w["iclr_base"],
                              w["w_iclr"], w["w_gate"], w["k_k"], w["k_a"], w["r_k"], w["lnx_g"],
                              w["lnx_b"], w["w_rwkv_out"], nb=nb, tt=tt_rwkv, t_valid=min(t_valid, tt_rwkv))
    y = _merge_ffn(x2, proj, o_a.reshape(b * t, d), o_b.reshape(b * t, d), w["w_out"], w["post_mix_g"],
                   w["pre_ffn_g"], w["w_ff_up"], w["w_ff_down"], w["post_ffn_g"], tm=tm)
    return y.reshape(b, t, d), tail, s_new


def kernel(x_prompt, x_sample, state_conv, state_shift, state_wkv, pre_mix_g, post_mix_g, pre_ffn_g, post_ffn_g, w_in, conv_w, conv_b, conv_ln_g, conv_ln_b, w_conv_out, shift_mu, decay_base, w_decay_up, iclr_base, w_iclr_up, w_gate_up, k_k, k_a, r_k, lnx_g, lnx_b, w_rwkv_out, w_out, w_ff_up, w_ff_down):
    depth = w_in.shape[0]
    assert depth == 1
    d = x_prompt.shape[-1]
    n_heads = state_wkv.shape[2]
    r_dec, r_iclr, r_gate = w_decay_up.shape[1], w_iclr_up.shape[1], w_gate_up.shape[1]
    assert n_heads * HEAD_N == d and r_dec + r_iclr == LANES and r_gate == LANES
    n_lora = r_dec + r_iclr + r_gate
    i0 = 2 * d
    i1 = i0 + 3 * d + n_lora
    row = lambda a: a.reshape(1, -1)

    w_in_l = w_in[0]
    w_in_p = jnp.concatenate([w_in_l[:, :i0], w_in_l[:, i1:], w_in_l[:, i0:i1]], axis=1).astype(BF16)
    zpad = lambda n: jnp.zeros((n, d), F32)
    w = dict(
        pre_mix_g=row(pre_mix_g[0]), post_mix_g=row(post_mix_g[0]), pre_ffn_g=row(pre_ffn_g[0]),
        post_ffn_g=row(post_ffn_g[0]), w_in=w_in_p,
        conv_w=conv_w[0], conv_b=row(conv_b[0]), conv_ln_g=row(conv_ln_g[0]), conv_ln_b=row(conv_ln_b[0]),
        w_conv_out=w_conv_out[0].astype(BF16), mu=row(shift_mu[0]),
        decay_base=row(decay_base[0]), iclr_base=row(iclr_base[0]),
        w_dec=jnp.concatenate([w_decay_up[0], zpad(r_iclr)], axis=0).astype(BF16),
        w_iclr=jnp.concatenate([zpad(r_dec), w_iclr_up[0]], axis=0).astype(BF16),
        w_gate=w_gate_up[0].astype(BF16),
        k_k=row(k_k[0]), k_a=row(k_a[0]), r_k=row(r_k[0]), lnx_g=row(lnx_g[0]), lnx_b=row(lnx_b[0]),
        w_rwkv_out=w_rwkv_out[0].astype(BF16), w_out=w_out[0].astype(BF16),
        w_ff_up=w_ff_up[0].astype(BF16), w_ff_down=w_ff_down[0].astype(BF16),
    )

    bp, tp, _ = x_prompt.shape
    yp, conv_p, s_p = _layer(
        x_prompt, jnp.zeros((bp, conv_w.shape[1] - 1, d), F32), jnp.zeros((bp, 1, 3 * d + n_lora), F32),
        jnp.zeros((bp, n_heads, HEAD_N, HEAD_N), F32), w, t_valid=tp, tt=256, tt_rwkv=512, tm=256, tm_in=512, nb=1)
    shift_p = _rms_rows(x_prompt[:, -1], w["pre_mix_g"])

    bs, ts, _ = x_sample.shape
    t_pad = 8
    xs = jnp.pad(x_sample, ((0, 0), (0, t_pad - ts), (0, 0)))
    prev0 = _norm_matmul(state_shift[0], w["pre_mix_g"], w_in_p[:, 4 * d:], tm=bs, tn=3 * d + n_lora,
                         apply_norm=False, name="shift_proj")
    ys, conv_s, s_s = _layer(
        xs, state_conv[0], prev0.reshape(bs, 1, -1), state_wkv[0], w,
        t_valid=ts, tt=t_pad, tt_rwkv=t_pad, tm=256, tm_in=512, nb=8)
    shift_s = _rms_rows(x_sample[:, -1], w["pre_mix_g"])

    return (yp, ys[:, :ts], conv_p[None], shift_p[None], s_p[None],
            conv_s[None], shift_s[None], s_s[None])
```

```python
import functools
import math

import jax
import jax.numpy as jnp
from jax import lax
from jax.experimental import pallas as pl
from jax.experimental.pallas import tpu as pltpu

F32 = jnp.float32
BF16 = jnp.bfloat16

RMS_EPS = 1e-6
LN_EPS = 1e-5
GN_EPS = 64e-5
HEAD_N = 64
LANES = 128
SUBLANES = 8
MXU_N = 256
CHUNK = 64
CONV_HALO = 32
VMEM_LIMIT = 52 * 1024 * 1024


def _cparams(n_axes):
    return pltpu.CompilerParams(dimension_semantics=("arbitrary",) * n_axes,
                                vmem_limit_bytes=VMEM_LIMIT)


def _rms(x, g):
    return x * lax.rsqrt(jnp.mean(x * x, axis=-1, keepdims=True) + RMS_EPS) * g


def _sigmoid(x):
    return 0.5 * jnp.tanh(0.5 * x) + 0.5


def _dot(a, b):
    return jnp.dot(a, b, preferred_element_type=F32)


def _dot_nt(a, b):
    return lax.dot_general(a, b, (((1,), (1,)), ((), ())), preferred_element_type=F32)


def _dot_tn(a, b):
    return lax.dot_general(a, b, (((0,), (0,)), ((), ())), preferred_element_type=F32)


def _proj_kernel(x_ref, g_ref, w_ref, o_ref, *, apply_norm):
    x = x_ref[...]
    if apply_norm:
        x = _rms(x, g_ref[...])
    o_ref[...] = _dot(x.astype(BF16), w_ref[...])


def _norm_matmul(x, g, w, *, tm, tn, apply_norm, name):
    m, d = x.shape
    n = w.shape[1]
    assert m % tm == 0 and n % tn == 0
    return pl.pallas_call(
        functools.partial(_proj_kernel, apply_norm=apply_norm),
        grid=(n // tn, m // tm),
        in_specs=[pl.BlockSpec((tm, d), lambda j, i: (i, 0)),
                  pl.BlockSpec((1, d), lambda j, i: (0, 0)),
                  pl.BlockSpec((d, tn), lambda j, i: (0, j))],
        out_specs=pl.BlockSpec((tm, tn), lambda j, i: (i, j)),
        out_shape=jax.ShapeDtypeStruct((m, n), F32),
        compiler_params=_cparams(2),
        name=name,
    )(x, g, w)


def _rms_rows_kernel(x_ref, g_ref, o_ref):
    o_ref[...] = _rms(x_ref[...], g_ref[...])


def _rms_rows(x, g):
    return pl.pallas_call(_rms_rows_kernel, out_shape=jax.ShapeDtypeStruct(x.shape, F32),
                          name="shift_rows")(x, g)


def _conv_kernel(val_ref, gate_ref, st_ref, cw_ref, cb_ref, lg_ref, lb_ref, wo_ref,
                 oa_ref, tail_ref, ext_ref, carry_ref, c_ref, wb_ref, *, nb, tt, t_valid, rb, n_taps):
    halo = n_taps - 1
    lead = CONV_HALO - halo
    rows = CONV_HALO + tt
    d = cw_ref.shape[1]

    @pl.when(pl.program_id(1) == 0)
    def _():
        carry_ref[...] = st_ref[...]

    @pl.when((pl.program_id(0) == 0) & (pl.program_id(1) == 0))
    def _():
        for j in range(n_taps):
            wb_ref[j * SUBLANES:(j + 1) * SUBLANES, :] = jnp.broadcast_to(cw_ref[j:j + 1, :], (SUBLANES, d))

    for i in range(nb):
        ext_ref[0, 0:lead, :] = jnp.zeros((lead, d), F32)
        ext_ref[0, lead:CONV_HALO, :] = carry_ref[i]
        ext_ref[0, CONV_HALO:rows, :] = val_ref[i] * _sigmoid(gate_ref[i])
        for s in range(1, SUBLANES):
            ext_ref[s, 0:rows - SUBLANES, :] = ext_ref[0, s:s + rows - SUBLANES, :]
        for r0 in range(0, tt, rb):
            acc = jnp.broadcast_to(cb_ref[...], (rb, d))
            for j in range(n_taps):
                q, s = divmod(lead + j, SUBLANES)
                a = q * SUBLANES + r0
                w8 = wb_ref[j * SUBLANES:(j + 1) * SUBLANES, :]
                acc = acc + ext_ref[s, a:a + rb, :] * jnp.concatenate([w8] * (rb // SUBLANES), axis=0)
            c_ref[i * tt + r0:i * tt + r0 + rb, :] = acc
        tail = ext_ref[0, lead + t_valid:lead + t_valid + halo, :]
        carry_ref[i] = tail
        tail_ref[i] = tail
    c = c_ref[...]
    mu = jnp.mean(c, axis=-1, keepdims=True)
    cc = c - mu
    var = jnp.mean(cc * cc, axis=-1, keepdims=True)
    z = cc * lax.rsqrt(var + LN_EPS) * lg_ref[...] + lb_ref[...]
    z = z * _sigmoid(z)
    out = _dot(z.astype(BF16), wo_ref[...])
    for i in range(nb):
        oa_ref[i] = out[i * tt:(i + 1) * tt]


def _conv_branch(proj, conv_state, conv_w, conv_b, ln_g, ln_b, w_conv_out, *, nb, tt, t_valid):
    b, t, _ = proj.shape
    d = conv_w.shape[1]
    n_taps = conv_w.shape[0]
    assert b % nb == 0 and t % tt == 0 and (nb == 1 or t == tt)
    rb = min(tt, 16)
    kern = functools.partial(_conv_kernel, nb=nb, tt=tt, t_valid=t_valid, rb=rb, n_taps=n_taps)
    row = lambda shape: pl.BlockSpec(shape, lambda i, j: (0, 0))
    return pl.pallas_call(
        kern,
        grid=(b // nb, t // tt),
        in_specs=[pl.BlockSpec((nb, tt, d), lambda i, j: (i, j, 0)),
                  pl.BlockSpec((nb, tt, d), lambda i, j: (i, j, 1)),
                  pl.BlockSpec((nb, n_taps - 1, d), lambda i, j: (i, 0, 0)),
                  row((n_taps, d)), row((1, d)), row((1, d)), row((1, d)), row((d, d))],
        out_specs=[pl.BlockSpec((nb, tt, d), lambda i, j: (i, j, 0)),
                   pl.BlockSpec((nb, n_taps - 1, d), lambda i, j: (i, 0, 0))],
        out_shape=[jax.ShapeDtypeStruct((b, t, d), F32),
                   jax.ShapeDtypeStruct((b, n_taps - 1, d), F32)],
        scratch_shapes=[pltpu.VMEM((SUBLANES, CONV_HALO + tt, d), F32),
                        pltpu.VMEM((nb, n_taps - 1, d), F32),
                        pltpu.VMEM((nb * tt, d), F32),
                        pltpu.VMEM((n_taps * SUBLANES, d), F32)],
        compiler_params=_cparams(2),
        name="conv_branch",
    )(proj, proj, conv_state, conv_w, conv_b, ln_g, ln_b, w_conv_out)


def _split3(x):
    hi = x.astype(BF16)
    r1 = x - hi.astype(F32)
    mid = r1.astype(BF16)
    lo = (r1 - mid.astype(F32)).astype(BF16)
    return hi, mid, lo


def _rwkv_kernel(r_ref, k_ref, v_ref, lo_ref, prev_ref, s0_ref,
                 mu_ref, dbase_ref, wdec_ref, ibase_ref, wiclr_ref, wgate_ref,
                 kkw_ref, ka_ref, rk_ref, lng_ref, lnb_ref, wout_ref,
                 ob_ref, sout_ref,
                 sh_ref, at_b, rt_b, bh_b, kh_b, be_b, ke_b, v_b, dall_s, g_s, bo_s,
                 tm_s, arb_s, lv_s, yv_s, y_s, sbd_ref,
                 *, nb, tt, t_valid, n_sq):
    d = r_ref.shape[2]
    n_lora = lo_ref.shape[2]
    n_pairs = d // LANES
    rows_t = nb * tt
    rows_s = at_b.shape[0]
    n_chunks = rows_s // CHUNK
    seg = min(tt, CHUNK)
    sub = min(rows_t, 2 * CHUNK)
    ti = pl.program_id(1)

    @pl.when(ti == 0)
    def _():
        zh = jnp.zeros((HEAD_N, HEAD_N), F32)
        for i in range(nb):
            for p in range(n_pairs):
                sbd_ref[i, p] = jnp.concatenate(
                    [jnp.concatenate([s0_ref[i, 2 * p], zh], axis=1),
                     jnp.concatenate([zh, s0_ref[i, 2 * p + 1]], axis=1)], axis=0)

    for i in range(nb):
        r0 = 8 + i * tt
        sh_ref[r0:r0 + tt, 0:d] = r_ref[i]
        sh_ref[r0:r0 + tt, d:2 * d] = k_ref[i]
        sh_ref[r0:r0 + tt, 2 * d:3 * d] = v_ref[i]
        sh_ref[r0:r0 + tt, 3 * d:3 * d + n_lora] = lo_ref[i]

    @pl.when(ti == 0)
    def _():
        for i in range(nb):
            sh_ref[7 + i * tt:8 + i * tt, :] = prev_ref[i]

    li = lax.broadcasted_iota(jnp.int32, (LANES, LANES), 0) // HEAD_N
    lj = lax.broadcasted_iota(jnp.int32, (LANES, LANES), 1) // HEAD_N
    same_head = li == lj
    wi = lax.broadcasted_iota(jnp.int32, (MXU_N, MXU_N), 0) // HEAD_N
    wj = lax.broadcasted_iota(jnp.int32, (MXU_N, MXU_N), 1) // HEAD_N
    ones_bd = jnp.where(wi == wj, 1.0, 0.0).astype(BF16)

    def head_sum(x):
        parts = [_dot(x[:, c0:c0 + MXU_N].astype(BF16), ones_bd) for c0 in range(0, d, MXU_N)]
        return jnp.concatenate(parts, axis=1)

    ri = lax.broadcasted_iota(jnp.int32, (sub, sub), 0)
    rj = lax.broadcasted_iota(jnp.int32, (sub, sub), 1)
    same_seg = (ri // seg) == (rj // seg)
    tril_blk = jnp.where(rj <= ri, jnp.where(same_seg, 1.0, 0.0), 0.0).astype(BF16)

    def prep(q0):
        def mixed(c0, c1):
            cur = sh_ref[8 + q0:8 + q0 + sub, c0:c1]
            prv = sh_ref[7 + q0:7 + q0 + sub, c0:c1]
            return cur + (prv - cur) * mu_ref[:, c0:c1]

        lora = mixed(3 * d, 3 * d + n_lora)
        l_di = lora[:, 0:LANES]
        l_g = lora[:, LANES:2 * LANES]
        dec_in = dbase_ref[...] + _dot(jnp.tanh(l_di).astype(BF16), wdec_ref[...])
        a = _sigmoid(ibase_ref[...] + _dot(l_di.astype(BF16), wiclr_ref[...]))
        g_s[q0:q0 + sub, :] = _dot(_sigmoid(l_g).astype(BF16), wgate_ref[...])
        yield
        z = -dec_in
        softplus = jnp.maximum(z, 0.0) + jnp.log(1.0 + jnp.exp(-jnp.abs(z)))
        lw = -jnp.exp(-softplus - 0.5)
        live = None
        if t_valid < tt:
            live = (lax.broadcasted_iota(jnp.int32, (sub, d), 0) + q0) % tt < t_valid
            lw = jnp.where(live, lw, 0.0)
        hi, mid, lo = _split3(lw)
        cum = _dot(tril_blk, hi) + _dot(tril_blk, mid) + _dot(tril_blk, lo)
        tot = jnp.concatenate([jnp.broadcast_to(cum[e - 1:e, :], (seg, d)) for e in range(seg, sub + 1, seg)],
                              axis=0)
        dec_all = jnp.exp(tot)
        for sg in range(sub // seg):
            dall_s[q0 // seg + sg:q0 // seg + sg + 1, :] = dec_all[sg * seg:sg * seg + 1, :]
        yield

        def put(ref, val, mask=True):
            if mask and live is not None:
                val = jnp.where(live, val, 0.0)
            ref[q0:q0 + sub, :] = val.astype(ref.dtype)

        k = mixed(d, 2 * d)
        kk = k * kkw_ref[...]
        kk = kk * lax.rsqrt(jnp.maximum(head_sum(kk * kk), 1e-24))
        yield
        put(at_b, -kk * jnp.exp(cum - lw))
        beta = kk * a
        put(bh_b, beta * jnp.exp(-cum))
        yield
        put(be_b, beta * jnp.exp(tot - cum))
        kt = k * (1.0 + (a - 1.0) * ka_ref[...])
        yield
        put(kh_b, kt * jnp.exp(-cum))
        put(ke_b, kt * jnp.exp(tot - cum))
        yield
        r = mixed(0, d)
        v = mixed(2 * d, 3 * d)
        bo_s[q0:q0 + sub, :] = head_sum(r * kt * rk_ref[...]) * v
        yield
        put(rt_b, r * jnp.exp(cum), mask=False)
        put(v_b, v)
        yield

    pt = lax.broadcasted_iota(jnp.int32, (CHUNK, LANES), 0)
    pl_ = lax.broadcasted_iota(jnp.int32, (CHUNK, LANES), 1)
    ps = pl_ % HEAD_N
    strict = ps < pt
    incl = ps <= pt
    if seg < CHUNK:
        own = (ps // seg) == (pt // seg)
        strict = jnp.logical_and(own, strict)
        incl = jnp.logical_and(own, incl)
    lane_lo = pl_ < HEAD_N
    eye_pair = jnp.where(ps == pt, 1.0, 0.0)
    zc = jnp.zeros((CHUNK, LANES), F32)

    in_lo = jnp.where(lane_lo, 1.0, 0.0).astype(BF16) > 0
    zb = jnp.zeros((CHUNK, LANES), BF16)

    def bd(xb):
        return jnp.concatenate([jnp.where(in_lo, xb, zb), jnp.where(in_lo, zb, xb)], axis=0)

    def blk(ref, c, p):
        return ref[c * CHUNK:(c + 1) * CHUNK, p * LANES:(p + 1) * LANES]

    def blkb(ref, c, p):
        return blk(ref, c, p).astype(BF16)

    def put_blk(ref, c, p, val):
        ref[c * CHUNK:(c + 1) * CHUNK, p * LANES:(p + 1) * LANES] = val

    def chunk_setup(its):
        ar = [_dot_nt(jnp.concatenate([blkb(at_b, c, p), blkb(rt_b, c, p)], axis=0),
                      jnp.concatenate([bd(blkb(bh_b, c, p)), bd(blkb(kh_b, c, p))], axis=0))
              for c, p in its]
        yield
        l_ab = [jnp.where(strict, a[0:CHUNK, 0:LANES], zc) for a in ar]
        for (c, p), a in zip(its, ar):
            put_blk(arb_s, c, p, jnp.where(incl, a[CHUNK:2 * CHUNK, 0:LANES], zc).astype(BF16))
        lv_yv = [_dot(jnp.concatenate([jnp.where(strict, a[0:CHUNK, LANES:2 * LANES], zc),
                                       jnp.where(incl, a[CHUNK:2 * CHUNK, LANES:2 * LANES], zc)],
                                      axis=0).astype(BF16), bd(blkb(v_b, c, p)))
                 for (c, p), a in zip(its, ar)]
        yield
        for (c, p), m in zip(its, lv_yv):
            put_blk(lv_s, c, p, m[0:CHUNK])
            put_blk(yv_s, c, p, m[CHUNK:2 * CHUNK])
        tm = [eye_pair + l for l in l_ab]
        if n_sq > 0:
            pw = [_dot(l.astype(BF16), bd(l.astype(BF16))) for l in l_ab]
            yield
        for i in range(n_sq):
            if i == n_sq - 1:
                tm = [t + _dot(q.astype(BF16), bd(t.astype(BF16))) for t, q in zip(tm, pw)]
            else:
                pr = [_dot(q.astype(BF16), jnp.concatenate([bd(q.astype(BF16)), bd(t.astype(BF16))], axis=1))
                      for t, q in zip(tm, pw)]
                pw = [x[:, 0:LANES] for x in pr]
                tm = [t + x[:, LANES:2 * LANES] for t, x in zip(tm, pr)]
            yield
        for (c, p), t in zip(its, tm):
            put_blk(tm_s, c, p, t.astype(BF16))
        yield

    pairs = range(n_pairs)
    segs = range(CHUNK // seg)

    def cat_rows(parts):
        return parts[0] if len(parts) == 1 else jnp.concatenate(parts, axis=0)

    def stack_bf16(top, bot):
        if seg % 16 == 0:
            return jnp.concatenate([top.astype(BF16), bot.astype(BF16)], axis=0)
        return jnp.concatenate([top.astype(F32), bot.astype(F32)], axis=0).astype(BF16)

    def chunk_step(c):
        def seq_of(sg):
            return (c * CHUNK + sg * seg) // tt if nb > 1 else 0

        def seg_rows(ref, sg, p):
            r0 = c * CHUNK + sg * seg
            return ref[r0:r0 + seg, p * LANES:(p + 1) * LANES]

        s_old = [[sbd_ref[seq_of(sg), p] for p in pairs] for sg in segs]
        x = [[_dot_nt(stack_bf16(seg_rows(at_b, sg, p), seg_rows(rt_b, sg, p)), s_old[sg][p].astype(BF16))
              for p in pairs] for sg in segs]
        yield
        x_a = [cat_rows([x[sg][p][0:seg] for sg in segs]) for p in pairs]
        x_r = [cat_rows([x[sg][p][seg:2 * seg] for sg in segs]) for p in pairs]
        u = [_dot(blk(tm_s, c, p), bd((x_a[p] + blk(lv_s, c, p)).astype(BF16))) for p in pairs]
        yield
        ub = [t.astype(BF16) for t in u]
        for p in pairs:
            put_blk(y_s, c, p, x_r[p] + blk(yv_s, c, p) + _dot(blk(arb_s, c, p), bd(ub[p])))
        yield
        for sg in segs:
            upd = [_dot_tn(stack_bf16(u[p][sg * seg:(sg + 1) * seg], seg_rows(v_b, sg, p)),
                           stack_bf16(seg_rows(be_b, sg, p), seg_rows(ke_b, sg, p))) for p in pairs]
            gi = c * (CHUNK // seg) + sg
            for p in pairs:
                sbd_ref[seq_of(sg), p] = (s_old[sg][p] * dall_s[gi:gi + 1, p * LANES:(p + 1) * LANES]
                                          + jnp.where(same_head, upd[p], 0.0))
            yield

    def finish(q0):
        y_n = []
        for c0 in range(0, d, MXU_N):
            yb = y_s[q0:q0 + sub, c0:c0 + MXU_N]
            yc = yb - _dot(yb.astype(BF16), ones_bd) * (1.0 / HEAD_N)
            var = _dot((yc * yc).astype(BF16), ones_bd) * (1.0 / HEAD_N)
            y_n.append(yc * lax.rsqrt(var + GN_EPS))
            yield
        y_n = jnp.concatenate(y_n, axis=1) * lng_ref[...] + lnb_ref[...]
        out = _dot(((y_n + bo_s[q0:q0 + sub, :]) * g_s[q0:q0 + sub, :]).astype(BF16), wout_ref[...])
        if nb == 1:
            ob_ref[0, q0:q0 + sub, :] = out
        else:
            for i in range(sub // tt):
                ob_ref[q0 // tt + i] = out[i * tt:(i + 1) * tt]
        yield

    cps = sub // CHUNK
    n_sub = rows_t // sub

    def chunk_walk(s):
        for c in range(cps):
            yield from chunk_step(s * cps + c)

    def round_robin(gens):
        while gens:
            for gen in list(gens):
                if next(gen, StopIteration) is StopIteration:
                    gens.remove(gen)

    for slot in range(n_sub + 3):
        live_gens = []
        if slot < n_sub:
            live_gens.append(prep(slot * sub))
        if 0 <= slot - 1 < n_sub:
            live_gens.append(chunk_setup([((slot - 1) * cps + c, p) for c in range(cps) for p in pairs]))
        if 0 <= slot - 2 < n_sub:
            live_gens.append(chunk_walk(slot - 2))
        if 0 <= slot - 3 < n_sub:
            live_gens.append(finish((slot - 3) * sub))
        round_robin(live_gens)
        if slot == 0 and nb == 1:
            sh_ref[7:8, :] = sh_ref[7 + t_valid:8 + t_valid, :]

    @pl.when(ti == pl.num_programs(1) - 1)
    def _():
        for i in range(nb):
            for p in range(n_pairs):
                sout_ref[i, 2 * p] = sbd_ref[i, p, 0:HEAD_N, 0:HEAD_N]
                sout_ref[i, 2 * p + 1] = sbd_ref[i, p, HEAD_N:2 * HEAD_N, HEAD_N:2 * HEAD_N]


def _rwkv_branch(proj, prev0, s0, mu, dbase, wdec, ibase, wiclr, wgate, kkw, ka, rk, lng, lnb, wout,
                 *, nb, tt, t_valid):
    b, t, _ = proj.shape
    d = wout.shape[0]
    n_lora = wgate.shape[0] * 2
    n_pairs = d // LANES
    rows_s = nb * tt
    seg = min(tt, CHUNK)
    assert b % nb == 0 and t % tt == 0 and rows_s % CHUNK == 0 and max(tt, CHUNK) % seg == 0
    assert nb == 1 or (t == tt and t_valid < tt)
    n_sq = max(0, math.ceil(math.log2(min(seg, t_valid))) - 1)
    kern = functools.partial(_rwkv_kernel, nb=nb, tt=tt, t_valid=t_valid, n_sq=n_sq)
    const = lambda shape: pl.BlockSpec(shape, lambda i, j: (0,) * len(shape))
    col = lambda c: pl.BlockSpec((nb, tt, d), lambda i, j: (i, j, c))
    big = lambda dt: pltpu.VMEM((rows_s, d), dt)
    op_dt = BF16 if seg % 16 == 0 else F32
    return pl.pallas_call(
        kern,
        grid=(b // nb, t // tt),
        in_specs=[col(4), col(5), col(6),
                  pl.BlockSpec((nb, tt, n_lora), lambda i, j: (i, j, 7 * d // n_lora)),
                  pl.BlockSpec((nb, 1, 3 * d + n_lora), lambda i, j: (i, 0, 0)),
                  pl.BlockSpec((nb, 2 * n_pairs, HEAD_N, HEAD_N), lambda i, j: (i, 0, 0, 0)),
                  const((1, 3 * d + n_lora)), const((1, d)), const((LANES, d)), const((1, d)),
                  const((LANES, d)), const((LANES, d)), const((1, d)), const((1, d)), const((1, d)),
                  const((1, d)), const((1, d)), const((d, d))],
        out_specs=[pl.BlockSpec((nb, tt, d), lambda i, j: (i, j, 0)),
                   pl.BlockSpec((nb, 2 * n_pairs, HEAD_N, HEAD_N), lambda i, j: (i, 0, 0, 0))],
        out_shape=[jax.ShapeDtypeStruct((b, t, d), F32),
                   jax.ShapeDtypeStruct((b, 2 * n_pairs, HEAD_N, HEAD_N), F32)],
        scratch_shapes=[pltpu.VMEM((8 + rows_s, 3 * d + n_lora), F32),
                        big(op_dt), big(op_dt), big(BF16), big(BF16), big(op_dt), big(op_dt), big(op_dt),
                        pltpu.VMEM((max(rows_s // seg, SUBLANES), d), F32),
                        big(F32), big(F32),
                        big(BF16), big(BF16), big(F32), big(F32), big(F32),
                        pltpu.VMEM((nb, n_pairs, LANES, LANES), F32)],
        compiler_params=_cparams(2),
        name="rwkv_branch",
    )(proj, proj, proj, proj, prev0, s0, mu, dbase, wdec, ibase, wiclr, wgate, kkw, ka, rk, lng, lnb, wout)


def _ffn_kernel(x_ref, ga_ref, gb_ref, oa_ref, ob_ref, wo_ref, g1_ref, g2_ref, wu_ref, wd_ref, g3_ref,
                o_ref, *, ff_chunk):
    merged = _sigmoid(ga_ref[...]) * oa_ref[...] + _sigmoid(gb_ref[...]) * ob_ref[...]
    m = _dot(merged.astype(BF16), wo_ref[...])
    x1 = x_ref[...] + _rms(m, g1_ref[...])
    h2 = _rms(x1, g2_ref[...]).astype(BF16)
    f = None
    for c0 in range(0, wu_ref.shape[1], ff_chunk):
        up = jnp.maximum(_dot(h2, wu_ref[:, c0:c0 + ff_chunk]), 0.0)
        part = _dot((up * up).astype(BF16), wd_ref[c0:c0 + ff_chunk, :])
        f = part if f is None else f + part
    o_ref[...] = x1 + _rms(f, g3_ref[...])


def _merge_ffn(x, proj, o_a, o_b, w_out, g1, g2, w_up, w_down, g3, *, tm):
    m, d = x.shape
    dff = w_up.shape[1]
    tok = lambda c: pl.BlockSpec((tm, d), lambda i: (i, c))
    const = lambda shape: pl.BlockSpec(shape, lambda i: (0, 0), pipeline_mode=pl.Buffered(1))
    return pl.pallas_call(
        functools.partial(_ffn_kernel, ff_chunk=d),
        grid=(m // tm,),
        in_specs=[tok(0), tok(2), tok(3), tok(0), tok(0),
                  const((d, d)), const((1, d)), const((1, d)), const((d, dff)), const((dff, d)),
                  const((1, d))],
        out_specs=tok(0),
        out_shape=jax.ShapeDtypeStruct((m, d), F32),
        compiler_params=_cparams(1),
        name="merge_ffn",
    )(x, proj, proj, o_a, o_b, w_out, g1, g2, w_up, w_down, g3)


def _layer(x, conv_state, prev0, s0, w, *, t_valid, tt, tt_rwkv, tm, tm_in, nb):
    b, t, d = x.shape
    x2 = x.reshape(b * t, d)
    tm, tm_in = min(tm, b * t), min(tm_in, b * t)
    proj = _norm_matmul(x2, w["pre_mix_g"], w["w_in"], tm=tm_in, tn=w["w_in"].shape[1] // 2,
                        apply_norm=True, name="in_proj")
    proj3 = proj.reshape(b, t, -1)
    o_b, s_new = _rwkv_branch(proj3, prev0, s0, w["mu"], w["decay_base"], w["w_dec"], w["iclr_base"],
                              w["w_iclr"], w["w_gate"], w["k_k"], w["k_a"], w["r_k"], w["lnx_g"],
                              w["lnx_b"], w["w_rwkv_out"], nb=nb, tt=tt_rwkv, t_valid=min(t_valid, tt_rwkv))
    o_a, tail = _conv_branch(proj3, conv_state, w["conv_w"], w["conv_b"], w["conv_ln_g"], w["conv_ln_b"],
                             w["w_conv_out"], nb=nb, tt=tt, t_valid=min(t_valid, tt))
    y = _merge_ffn(x2, proj, o_a.reshape(b * t, d), o_b.reshape(b * t, d), w["w_out"], w["post_mix_g"],
                   w["pre_ffn_g"], w["w_ff_up"], w["w_ff_down"], w["post_ffn_g"], tm=tm)
    return y.reshape(b, t, d), tail, s_new


def kernel(x_prompt, x_sample, state_conv, state_shift, state_wkv, pre_mix_g, post_mix_g, pre_ffn_g, post_ffn_g, w_in, conv_w, conv_b, conv_ln_g, conv_ln_b, w_conv_out, shift_mu, decay_base, w_decay_up, iclr_base, w_iclr_up, w_gate_up, k_k, k_a, r_k, lnx_g, lnx_b, w_rwkv_out, w_out, w_ff_up, w_ff_down):
    depth = w_in.shape[0]
    assert depth == 1
    d = x_prompt.shape[-1]
    n_heads = state_wkv.shape[2]
    r_dec, r_iclr, r_gate = w_decay_up.shape[1], w_iclr_up.shape[1], w_gate_up.shape[1]
    assert n_heads * HEAD_N == d and r_dec + r_iclr == LANES and r_gate == LANES
    n_lora = r_dec + r_iclr + r_gate
    i0 = 2 * d
    i1 = i0 + 3 * d + n_lora
    row = lambda a: a.reshape(1, -1)

    w_in_l = w_in[0]
    w_in_p = jnp.concatenate([w_in_l[:, :i0], w_in_l[:, i1:], w_in_l[:, i0:i1]], axis=1).astype(BF16)
    zpad = lambda n: jnp.zeros((n, d), F32)
    w = dict(
        pre_mix_g=row(pre_mix_g[0]), post_mix_g=row(post_mix_g[0]), pre_ffn_g=row(pre_ffn_g[0]),
        post_ffn_g=row(post_ffn_g[0]), w_in=w_in_p,
        conv_w=conv_w[0], conv_b=row(conv_b[0]), conv_ln_g=row(conv_ln_g[0]), conv_ln_b=row(conv_ln_b[0]),
        w_conv_out=w_conv_out[0].astype(BF16), mu=row(shift_mu[0]),
        decay_base=row(decay_base[0]), iclr_base=row(iclr_base[0]),
        w_dec=jnp.concatenate([w_decay_up[0], zpad(r_iclr)], axis=0).astype(BF16),
        w_iclr=jnp.concatenate([zpad(r_dec), w_iclr_up[0]], axis=0).astype(BF16),
        w_gate=w_gate_up[0].astype(BF16),
        k_k=row(k_k[0]), k_a=row(k_a[0]), r_k=row(r_k[0]), lnx_g=row(lnx_g[0]), lnx_b=row(lnx_b[0]),
        w_rwkv_out=w_rwkv_out[0].astype(BF16), w_out=w_out[0].astype(BF16),
        w_ff_up=w_ff_up[0].astype(BF16), w_ff_down=w_ff_down[0].astype(BF16),
    )

    bp, tp, _ = x_prompt.shape
    yp, conv_p, s_p = _layer(
        x_prompt, jnp.zeros((bp, conv_w.shape[1] - 1, d), F32), jnp.zeros((bp, 1, 3 * d + n_lora), F32),
        jnp.zeros((bp, n_heads, HEAD_N, HEAD_N), F32), w, t_valid=tp, tt=256, tt_rwkv=512, tm=256, tm_in=512, nb=1)
    shift_p = _rms_rows(x_prompt[:, -1], w["pre_mix_g"])

    bs, ts, _ = x_sample.shape
    t_pad = 8
    xs = jnp.pad(x_sample, ((0, 0), (0, t_pad - ts), (0, 0)))
    prev0 = _norm_matmul(state_shift[0], w["pre_mix_g"], w_in_p[:, 4 * d:], tm=bs, tn=3 * d + n_lora,
                         apply_norm=False, name="shift_proj")
    ys, conv_s, s_s = _layer(
        xs, state_conv.reshape(state_conv.shape[1:]), prev0.reshape(bs, 1, -1),
        state_wkv.reshape(state_wkv.shape[1:]), w,
        t_valid=ts, tt=t_pad, tt_rwkv=t_pad, tm=256, tm_in=512, nb=8)
    shift_s = _rms_rows(x_sample[:, -1], w["pre_mix_g"])

    return (yp, ys[:, :ts], conv_p[None], shift_p[None], s_p[None],
            conv_s[None], shift_s[None], s_s[None])
```

```python
import functools
import math

import jax
import jax.numpy as jnp
from jax import lax
from jax.experimental import pallas as pl
from jax.experimental.pallas import tpu as pltpu

F32 = jnp.float32
BF16 = jnp.bfloat16

RMS_EPS = 1e-6
LN_EPS = 1e-5
GN_EPS = 64e-5
HEAD_N = 64
LANES = 128
SUBLANES = 8
MXU_N = 256
CHUNK = 64
CONV_HALO = 32
VMEM_LIMIT = 52 * 1024 * 1024


def _cparams(n_axes):
    return pltpu.CompilerParams(dimension_semantics=("arbitrary",) * n_axes,
                                vmem_limit_bytes=VMEM_LIMIT)


def _rms(x, g):
    return x * lax.rsqrt(jnp.mean(x * x, axis=-1, keepdims=True) + RMS_EPS) * g


def _sigmoid(x):
    return 0.5 * jnp.tanh(0.5 * x) + 0.5


def _dot(a, b):
    return jnp.dot(a, b, preferred_element_type=F32)


def _dot_nt(a, b):
    return lax.dot_general(a, b, (((1,), (1,)), ((), ())), preferred_element_type=F32)


def _dot_tn(a, b):
    return lax.dot_general(a, b, (((0,), (0,)), ((), ())), preferred_element_type=F32)


def _proj_kernel(x_ref, g_ref, w_ref, o_ref, *, apply_norm, glu_width):
    x = x_ref[...]
    if apply_norm:
        x = _rms(x, g_ref[...])
    acc = _dot(x.astype(BF16), w_ref[...])
    if glu_width:
        glu = acc[:, 0:glu_width] * _sigmoid(acc[:, glu_width:2 * glu_width])
        o_ref[:, 0:glu_width] = jnp.where(pl.program_id(0) == 0, glu, acc[:, 0:glu_width])
        o_ref[:, glu_width:] = acc[:, glu_width:]
    else:
        o_ref[...] = acc


def _norm_matmul(x, g, w, *, tm, tn, apply_norm, name, glu_width=0):
    m, d = x.shape
    n = w.shape[1]
    assert m % tm == 0 and n % tn == 0 and 2 * glu_width <= tn
    return pl.pallas_call(
        functools.partial(_proj_kernel, apply_norm=apply_norm, glu_width=glu_width),
        grid=(n // tn, m // tm),
        in_specs=[pl.BlockSpec((tm, d), lambda j, i: (i, 0)),
                  pl.BlockSpec((1, d), lambda j, i: (0, 0)),
                  pl.BlockSpec((d, tn), lambda j, i: (0, j))],
        out_specs=pl.BlockSpec((tm, tn), lambda j, i: (i, j)),
        out_shape=jax.ShapeDtypeStruct((m, n), F32),
        compiler_params=_cparams(2),
        name=name,
    )(x, g, w)


def _rms_rows_kernel(x_ref, g_ref, o_ref):
    o_ref[...] = _rms(x_ref[...], g_ref[...])


def _rms_rows(x, g):
    return pl.pallas_call(_rms_rows_kernel, out_shape=jax.ShapeDtypeStruct(x.shape, F32),
                          name="shift_rows")(x, g)


def _conv_kernel(u_ref, st_ref, cw_ref, cb_ref, lg_ref, lb_ref, wo_ref,
                 oa_ref, tail_ref, ext_ref, carry_ref, c_ref, wb_ref, *, nb, tt, t_valid, rb, n_taps):
    halo = n_taps - 1
    lead = CONV_HALO - halo
    rows = CONV_HALO + tt
    d = cw_ref.shape[1]

    @pl.when(pl.program_id(1) == 0)
    def _():
        carry_ref[...] = st_ref[...]

    @pl.when((pl.program_id(0) == 0) & (pl.program_id(1) == 0))
    def _():
        for j in range(n_taps):
            wb_ref[j * SUBLANES:(j + 1) * SUBLANES, :] = jnp.broadcast_to(cw_ref[j:j + 1, :], (SUBLANES, d))

    for i in range(nb):
        ext_ref[0, 0:lead, :] = jnp.zeros((lead, d), F32)
        ext_ref[0, lead:CONV_HALO, :] = carry_ref[i]
        ext_ref[0, CONV_HALO:rows, :] = u_ref[i]
        for s in range(1, SUBLANES):
            ext_ref[s, 0:rows - SUBLANES, :] = ext_ref[0, s:s + rows - SUBLANES, :]
        for r0 in range(0, tt, rb):
            acc = jnp.broadcast_to(cb_ref[...], (rb, d))
            for j in range(n_taps):
                q, s = divmod(lead + j, SUBLANES)
                a = q * SUBLANES + r0
                w8 = wb_ref[j * SUBLANES:(j + 1) * SUBLANES, :]
                acc = acc + ext_ref[s, a:a + rb, :] * jnp.concatenate([w8] * (rb // SUBLANES), axis=0)
            c_ref[i * tt + r0:i * tt + r0 + rb, :] = acc
        tail = ext_ref[0, lead + t_valid:lead + t_valid + halo, :]
        carry_ref[i] = tail
        tail_ref[i] = tail
    c = c_ref[...]
    mu = jnp.mean(c, axis=-1, keepdims=True)
    cc = c - mu
    var = jnp.mean(cc * cc, axis=-1, keepdims=True)
    z = cc * lax.rsqrt(var + LN_EPS) * lg_ref[...] + lb_ref[...]
    z = z * _sigmoid(z)
    out = _dot(z.astype(BF16), wo_ref[...])
    for i in range(nb):
        oa_ref[i] = out[i * tt:(i + 1) * tt]


def _conv_branch(proj, conv_state, conv_w, conv_b, ln_g, ln_b, w_conv_out, *, nb, tt, t_valid):
    b, t, _ = proj.shape
    d = conv_w.shape[1]
    n_taps = conv_w.shape[0]
    assert b % nb == 0 and t % tt == 0 and (nb == 1 or t == tt)
    rb = min(tt, 16)
    kern = functools.partial(_conv_kernel, nb=nb, tt=tt, t_valid=t_valid, rb=rb, n_taps=n_taps)
    row = lambda shape: pl.BlockSpec(shape, lambda i, j: (0, 0))
    return pl.pallas_call(
        kern,
        grid=(b // nb, t // tt),
        in_specs=[pl.BlockSpec((nb, tt, d), lambda i, j: (i, j, 0)),
                  pl.BlockSpec((nb, n_taps - 1, d), lambda i, j: (i, 0, 0)),
                  row((n_taps, d)), row((1, d)), row((1, d)), row((1, d)), row((d, d))],
        out_specs=[pl.BlockSpec((nb, tt, d), lambda i, j: (i, j, 0)),
                   pl.BlockSpec((nb, n_taps - 1, d), lambda i, j: (i, 0, 0))],
        out_shape=[jax.ShapeDtypeStruct((b, t, d), F32),
                   jax.ShapeDtypeStruct((b, n_taps - 1, d), F32)],
        scratch_shapes=[pltpu.VMEM((SUBLANES, CONV_HALO + tt, d), F32),
                        pltpu.VMEM((nb, n_taps - 1, d), F32),
                        pltpu.VMEM((nb * tt, d), F32),
                        pltpu.VMEM((n_taps * SUBLANES, d), F32)],
        compiler_params=_cparams(2),
        name="conv_branch",
    )(proj, conv_state, conv_w, conv_b, ln_g, ln_b, w_conv_out)


def _split3(x):
    hi = x.astype(BF16)
    r1 = x - hi.astype(F32)
    mid = r1.astype(BF16)
    lo = (r1 - mid.astype(F32)).astype(BF16)
    return hi, mid, lo


def _rwkv_kernel(r_ref, k_ref, v_ref, lo_ref, prev_ref, s0_ref,
                 mu_ref, dbase_ref, wdec_ref, ibase_ref, wiclr_ref, wgate_ref,
                 kkw_ref, ka_ref, rk_ref, lng_ref, lnb_ref, wout_ref,
                 ob_ref, sout_ref,
                 sh_ref, at_b, rt_b, bh_b, kh_b, be_b, ke_b, v_b, dall_s, g_s, bo_s,
                 tm_s, arb_s, lv_s, yv_s, y_s, sbd_ref,
                 *, nb, tt, t_valid, n_sq):
    d = r_ref.shape[2]
    n_lora = lo_ref.shape[2]
    n_pairs = d // LANES
    rows_t = nb * tt
    rows_s = at_b.shape[0]
    n_chunks = rows_s // CHUNK
    seg = min(tt, CHUNK)
    sub = min(rows_t, 2 * CHUNK)
    ti = pl.program_id(1)

    @pl.when(ti == 0)
    def _():
        zh = jnp.zeros((HEAD_N, HEAD_N), F32)
        for i in range(nb):
            for p in range(n_pairs):
                sbd_ref[i, p] = jnp.concatenate(
                    [jnp.concatenate([s0_ref[i, 2 * p], zh], axis=1),
                     jnp.concatenate([zh, s0_ref[i, 2 * p + 1]], axis=1)], axis=0)

    for i in range(nb):
        r0 = 8 + i * tt
        sh_ref[r0:r0 + tt, 0:d] = r_ref[i]
        sh_ref[r0:r0 + tt, d:2 * d] = k_ref[i]
        sh_ref[r0:r0 + tt, 2 * d:3 * d] = v_ref[i]
        sh_ref[r0:r0 + tt, 3 * d:3 * d + n_lora] = lo_ref[i]

    @pl.when(ti == 0)
    def _():
        for i in range(nb):
            sh_ref[7 + i * tt:8 + i * tt, :] = prev_ref[i]

    li = lax.broadcasted_iota(jnp.int32, (LANES, LANES), 0) // HEAD_N
    lj = lax.broadcasted_iota(jnp.int32, (LANES, LANES), 1) // HEAD_N
    same_head = li == lj
    wi = lax.broadcasted_iota(jnp.int32, (MXU_N, MXU_N), 0) // HEAD_N
    wj = lax.broadcasted_iota(jnp.int32, (MXU_N, MXU_N), 1) // HEAD_N
    ones_bd = jnp.where(wi == wj, 1.0, 0.0).astype(BF16)

    def head_sum(x):
        parts = [_dot(x[:, c0:c0 + MXU_N].astype(BF16), ones_bd) for c0 in range(0, d, MXU_N)]
        return jnp.concatenate(parts, axis=1)

    ri = lax.broadcasted_iota(jnp.int32, (sub, sub), 0)
    rj = lax.broadcasted_iota(jnp.int32, (sub, sub), 1)
    same_seg = (ri // seg) == (rj // seg)
    tril_blk = jnp.where(rj <= ri, jnp.where(same_seg, 1.0, 0.0), 0.0).astype(BF16)

    def prep(q0):
        def mixed(c0, c1):
            cur = sh_ref[8 + q0:8 + q0 + sub, c0:c1]
            prv = sh_ref[7 + q0:7 + q0 + sub, c0:c1]
            return cur + (prv - cur) * mu_ref[:, c0:c1]

        lora = mixed(3 * d, 3 * d + n_lora)
        l_di = lora[:, 0:LANES]
        l_g = lora[:, LANES:2 * LANES]
        dec_in = dbase_ref[...] + _dot(jnp.tanh(l_di).astype(BF16), wdec_ref[...])
        a = _sigmoid(ibase_ref[...] + _dot(l_di.astype(BF16), wiclr_ref[...]))
        g_s[q0:q0 + sub, :] = _dot(_sigmoid(l_g).astype(BF16), wgate_ref[...])
        yield
        z = -dec_in
        softplus = jnp.maximum(z, 0.0) + jnp.log(1.0 + jnp.exp(-jnp.abs(z)))
        lw = -jnp.exp(-softplus - 0.5)
        live = None
        if t_valid < tt:
            live = (lax.broadcasted_iota(jnp.int32, (sub, d), 0) + q0) % tt < t_valid
            lw = jnp.where(live, lw, 0.0)
        hi, mid, lo = _split3(lw)
        cum = _dot(tril_blk, hi) + _dot(tril_blk, mid) + _dot(tril_blk, lo)
        tot = jnp.concatenate([jnp.broadcast_to(cum[e - 1:e, :], (seg, d)) for e in range(seg, sub + 1, seg)],
                              axis=0)
        dec_all = jnp.exp(tot)
        for sg in range(sub // seg):
            dall_s[q0 // seg + sg:q0 // seg + sg + 1, :] = dec_all[sg * seg:sg * seg + 1, :]
        yield

        def put(ref, val, mask=True):
            if mask and live is not None:
                val = jnp.where(live, val, 0.0)
            ref[q0:q0 + sub, :] = val.astype(ref.dtype)

        k = mixed(d, 2 * d)
        kk = k * kkw_ref[...]
        kk = kk * lax.rsqrt(jnp.maximum(head_sum(kk * kk), 1e-24))
        yield
        put(at_b, -kk * jnp.exp(cum - lw))
        beta = kk * a
        put(bh_b, beta * jnp.exp(-cum))
        yield
        put(be_b, beta * jnp.exp(tot - cum))
        kt = k * (1.0 + (a - 1.0) * ka_ref[...])
        yield
        put(kh_b, kt * jnp.exp(-cum))
        put(ke_b, kt * jnp.exp(tot - cum))
        yield
        r = mixed(0, d)
        v = mixed(2 * d, 3 * d)
        bo_s[q0:q0 + sub, :] = head_sum(r * kt * rk_ref[...]) * v
        yield
        put(rt_b, r * jnp.exp(cum), mask=False)
        put(v_b, v)
        yield

    pt = lax.broadcasted_iota(jnp.int32, (CHUNK, LANES), 0)
    pl_ = lax.broadcasted_iota(jnp.int32, (CHUNK, LANES), 1)
    ps = pl_ % HEAD_N
    strict = ps < pt
    incl = ps <= pt
    if seg < CHUNK:
        own = (ps // seg) == (pt // seg)
        strict = jnp.logical_and(own, strict)
        incl = jnp.logical_and(own, incl)
    lane_lo = pl_ < HEAD_N
    eye_pair = jnp.where(ps == pt, 1.0, 0.0)
    zc = jnp.zeros((CHUNK, LANES), F32)

    in_lo = jnp.where(lane_lo, 1.0, 0.0).astype(BF16) > 0
    zb = jnp.zeros((CHUNK, LANES), BF16)

    def bd(xb):
        return jnp.concatenate([jnp.where(in_lo, xb, zb), jnp.where(in_lo, zb, xb)], axis=0)

    def blk(ref, c, p):
        return ref[c * CHUNK:(c + 1) * CHUNK, p * LANES:(p + 1) * LANES]

    def blkb(ref, c, p):
        return blk(ref, c, p).astype(BF16)

    def put_blk(ref, c, p, val):
        ref[c * CHUNK:(c + 1) * CHUNK, p * LANES:(p + 1) * LANES] = val

    def chunk_setup(its):
        ar = [_dot_nt(jnp.concatenate([blkb(at_b, c, p), blkb(rt_b, c, p)], axis=0),
                      jnp.concatenate([bd(blkb(bh_b, c, p)), bd(blkb(kh_b, c, p))], axis=0))
              for c, p in its]
        yield
        l_ab = [jnp.where(strict, a[0:CHUNK, 0:LANES], zc) for a in ar]
        for (c, p), a in zip(its, ar):
            put_blk(arb_s, c, p, jnp.where(incl, a[CHUNK:2 * CHUNK, 0:LANES], zc).astype(BF16))
        lv_yv = [_dot(jnp.concatenate([jnp.where(strict, a[0:CHUNK, LANES:2 * LANES], zc),
                                       jnp.where(incl, a[CHUNK:2 * CHUNK, LANES:2 * LANES], zc)],
                                      axis=0).astype(BF16), bd(blkb(v_b, c, p)))
                 for (c, p), a in zip(its, ar)]
        yield
        for (c, p), m in zip(its, lv_yv):
            put_blk(lv_s, c, p, m[0:CHUNK])
            put_blk(yv_s, c, p, m[CHUNK:2 * CHUNK])
        tm = [eye_pair + l for l in l_ab]
        if n_sq > 0:
            pw = [_dot(l.astype(BF16), bd(l.astype(BF16))) for l in l_ab]
            yield
        for i in range(n_sq):
            if i == n_sq - 1:
                tm = [t + _dot(q.astype(BF16), bd(t.astype(BF16))) for t, q in zip(tm, pw)]
            else:
                pr = [_dot(q.astype(BF16), jnp.concatenate([bd(q.astype(BF16)), bd(t.astype(BF16))], axis=1))
                      for t, q in zip(tm, pw)]
                pw = [x[:, 0:LANES] for x in pr]
                tm = [t + x[:, LANES:2 * LANES] for t, x in zip(tm, pr)]
            yield
        for (c, p), t in zip(its, tm):
            put_blk(tm_s, c, p, t.astype(BF16))
        yield

    pairs = range(n_pairs)
    segs = range(CHUNK // seg)

    def cat_rows(parts):
        return parts[0] if len(parts) == 1 else jnp.concatenate(parts, axis=0)

    def stack_bf16(top, bot):
        if seg % 16 == 0:
            return jnp.concatenate([top.astype(BF16), bot.astype(BF16)], axis=0)
        return jnp.concatenate([top.astype(F32), bot.astype(F32)], axis=0).astype(BF16)

    def chunk_step(c):
        def seq_of(sg):
            return (c * CHUNK + sg * seg) // tt if nb > 1 else 0

        def seg_rows(ref, sg, p):
            r0 = c * CHUNK + sg * seg
            return ref[r0:r0 + seg, p * LANES:(p + 1) * LANES]

        s_old = [[sbd_ref[seq_of(sg), p] for p in pairs] for sg in segs]
        x = [[_dot_nt(stack_bf16(seg_rows(at_b, sg, p), seg_rows(rt_b, sg, p)), s_old[sg][p].astype(BF16))
              for p in pairs] for sg in segs]
        yield
        x_a = [cat_rows([x[sg][p][0:seg] for sg in segs]) for p in pairs]
        x_r = [cat_rows([x[sg][p][seg:2 * seg] for sg in segs]) for p in pairs]
        u = [_dot(blk(tm_s, c, p), bd((x_a[p] + blk(lv_s, c, p)).astype(BF16))) for p in pairs]
        yield
        ub = [t.astype(BF16) for t in u]
        for p in pairs:
            put_blk(y_s, c, p, x_r[p] + blk(yv_s, c, p) + _dot(blk(arb_s, c, p), bd(ub[p])))
        yield
        for sg in segs:
            upd = [_dot_tn(stack_bf16(u[p][sg * seg:(sg + 1) * seg], seg_rows(v_b, sg, p)),
                           stack_bf16(seg_rows(be_b, sg, p), seg_rows(ke_b, sg, p))) for p in pairs]
            gi = c * (CHUNK // seg) + sg
            for p in pairs:
                sbd_ref[seq_of(sg), p] = (s_old[sg][p] * dall_s[gi:gi + 1, p * LANES:(p + 1) * LANES]
                                          + jnp.where(same_head, upd[p], 0.0))
            yield

    def finish(q0):
        y_n = []
        for c0 in range(0, d, MXU_N):
            yb = y_s[q0:q0 + sub, c0:c0 + MXU_N]
            yc = yb - _dot(yb.astype(BF16), ones_bd) * (1.0 / HEAD_N)
            var = _dot((yc * yc).astype(BF16), ones_bd) * (1.0 / HEAD_N)
            y_n.append(yc * lax.rsqrt(var + GN_EPS))
            yield
        y_n = jnp.concatenate(y_n, axis=1) * lng_ref[...] + lnb_ref[...]
        out = _dot(((y_n + bo_s[q0:q0 + sub, :]) * g_s[q0:q0 + sub, :]).astype(BF16), wout_ref[...])
        if nb == 1:
            ob_ref[0, q0:q0 + sub, :] = out
        else:
            for i in range(sub // tt):
                ob_ref[q0 // tt + i] = out[i * tt:(i + 1) * tt]
        yield

    cps = sub // CHUNK
    n_sub = rows_t // sub

    def chunk_walk(s):
        for c in range(cps):
            yield from chunk_step(s * cps + c)

    def round_robin(gens):
        while gens:
            for gen in list(gens):
                if next(gen, StopIteration) is StopIteration:
                    gens.remove(gen)

    for slot in range(n_sub + 3):
        live_gens = []
        if slot < n_sub:
            live_gens.append(prep(slot * sub))
        if 0 <= slot - 1 < n_sub:
            live_gens.append(chunk_setup([((slot - 1) * cps + c, p) for c in range(cps) for p in pairs]))
        if 0 <= slot - 2 < n_sub:
            live_gens.append(chunk_walk(slot - 2))
        if 0 <= slot - 3 < n_sub:
            live_gens.append(finish((slot - 3) * sub))
        round_robin(live_gens)
        if slot == 0 and nb == 1:
            sh_ref[7:8, :] = sh_ref[7 + t_valid:8 + t_valid, :]

    @pl.when(ti == pl.num_programs(1) - 1)
    def _():
        for i in range(nb):
            for p in range(n_pairs):
                sout_ref[i, 2 * p] = sbd_ref[i, p, 0:HEAD_N, 0:HEAD_N]
                sout_ref[i, 2 * p + 1] = sbd_ref[i, p, HEAD_N:2 * HEAD_N, HEAD_N:2 * HEAD_N]


def _rwkv_branch(proj, prev0, s0, mu, dbase, wdec, ibase, wiclr, wgate, kkw, ka, rk, lng, lnb, wout,
                 *, nb, tt, t_valid):
    b, t, _ = proj.shape
    d = wout.shape[0]
    n_lora = wgate.shape[0] * 2
    n_pairs = d // LANES
    rows_s = nb * tt
    seg = min(tt, CHUNK)
    assert b % nb == 0 and t % tt == 0 and rows_s % CHUNK == 0 and max(tt, CHUNK) % seg == 0
    assert nb == 1 or (t == tt and t_valid < tt)
    n_sq = max(0, math.ceil(math.log2(min(seg, t_valid))) - 1)
    kern = functools.partial(_rwkv_kernel, nb=nb, tt=tt, t_valid=t_valid, n_sq=n_sq)
    const = lambda shape: pl.BlockSpec(shape, lambda i, j: (0,) * len(shape))
    col = lambda c: pl.BlockSpec((nb, tt, d), lambda i, j: (i, j, c))
    big = lambda dt: pltpu.VMEM((rows_s, d), dt)
    op_dt = BF16 if seg % 16 == 0 else F32
    return pl.pallas_call(
        kern,
        grid=(b // nb, t // tt),
        in_specs=[col(4), col(5), col(6),
                  pl.BlockSpec((nb, tt, n_lora), lambda i, j: (i, j, 7 * d // n_lora)),
                  pl.BlockSpec((nb, 1, 3 * d + n_lora), lambda i, j: (i, 0, 0)),
                  pl.BlockSpec((nb, 2 * n_pairs, HEAD_N, HEAD_N), lambda i, j: (i, 0, 0, 0)),
                  const((1, 3 * d + n_lora)), const((1, d)), const((LANES, d)), const((1, d)),
                  const((LANES, d)), const((LANES, d)), const((1, d)), const((1, d)), const((1, d)),
                  const((1, d)), const((1, d)), const((d, d))],
        out_specs=[pl.BlockSpec((nb, tt, d), lambda i, j: (i, j, 0)),
                   pl.BlockSpec((nb, 2 * n_pairs, HEAD_N, HEAD_N), lambda i, j: (i, 0, 0, 0))],
        out_shape=[jax.ShapeDtypeStruct((b, t, d), F32),
                   jax.ShapeDtypeStruct((b, 2 * n_pairs, HEAD_N, HEAD_N), F32)],
        scratch_shapes=[pltpu.VMEM((8 + rows_s, 3 * d + n_lora), F32),
                        big(op_dt), big(op_dt), big(BF16), big(BF16), big(op_dt), big(op_dt), big(op_dt),
                        pltpu.VMEM((max(rows_s // seg, SUBLANES), d), F32),
                        big(F32), big(F32),
                        big(BF16), big(BF16), big(F32), big(F32), big(F32),
                        pltpu.VMEM((nb, n_pairs, LANES, LANES), F32)],
        compiler_params=_cparams(2),
        name="rwkv_branch",
    )(proj, proj, proj, proj, prev0, s0, mu, dbase, wdec, ibase, wiclr, wgate, kkw, ka, rk, lng, lnb, wout)


def _ffn_kernel(x_ref, ga_ref, gb_ref, oa_ref, ob_ref, wo_ref, g1_ref, g2_ref, wu_ref, wd_ref, g3_ref,
                o_ref, *, ff_chunk):
    merged = _sigmoid(ga_ref[...]) * oa_ref[...] + _sigmoid(gb_ref[...]) * ob_ref[...]
    m = _dot(merged.astype(BF16), wo_ref[...])
    x1 = x_ref[...] + _rms(m, g1_ref[...])
    h2 = _rms(x1, g2_ref[...]).astype(BF16)
    f = None
    for c0 in range(0, wu_ref.shape[1], ff_chunk):
        up = jnp.maximum(_dot(h2, wu_ref[:, c0:c0 + ff_chunk]), 0.0)
        part = _dot((up * up).astype(BF16), wd_ref[c0:c0 + ff_chunk, :])
        f = part if f is None else f + part
    o_ref[...] = x1 + _rms(f, g3_ref[...])


def _merge_ffn(x, proj, o_a, o_b, w_out, g1, g2, w_up, w_down, g3, *, tm):
    m, d = x.shape
    dff = w_up.shape[1]
    tok = lambda c: pl.BlockSpec((tm, d), lambda i: (i, c))
    const = lambda shape: pl.BlockSpec(shape, lambda i: (0, 0), pipeline_mode=pl.Buffered(1))
    return pl.pallas_call(
        functools.partial(_ffn_kernel, ff_chunk=d),
        grid=(m // tm,),
        in_specs=[tok(0), tok(2), tok(3), tok(0), tok(0),
                  const((d, d)), const((1, d)), const((1, d)), const((d, dff)), const((dff, d)),
                  const((1, d))],
        out_specs=tok(0),
        out_shape=jax.ShapeDtypeStruct((m, d), F32),
        compiler_params=_cparams(1),
        name="merge_ffn",
    )(x, proj, proj, o_a, o_b, w_out, g1, g2, w_up, w_down, g3)


def _layer(x, conv_state, prev0, s0, w, *, t_valid, tt, tt_rwkv, tm, tm_in, nb):
    b, t, d = x.shape
    x2 = x.reshape(b * t, d)
    tm, tm_in = min(tm, b * t), min(tm_in, b * t)
    proj = _norm_matmul(x2, w["pre_mix_g"], w["w_in"], tm=tm_in, tn=w["w_in"].shape[1] // 2,
                        apply_norm=True, name="in_proj", glu_width=d)
    proj3 = proj.reshape(b, t, -1)
    o_b, s_new = _rwkv_branch(proj3, prev0, s0, w["mu"], w["decay_base"], w["w_dec"], w["iclr_base"],
                              w["w_iclr"], w["w_gate"], w["k_k"], w["k_a"], w["r_k"], w["lnx_g"],
                              w["lnx_b"], w["w_rwkv_out"], nb=nb, tt=tt_rwkv, t_valid=min(t_valid, tt_rwkv))
    o_a, tail = _conv_branch(proj3, conv_state, w["conv_w"], w["conv_b"], w["conv_ln_g"], w["conv_ln_b"],
                             w["w_conv_out"], nb=nb, tt=tt, t_valid=min(t_valid, tt))
    y = _merge_ffn(x2, proj, o_a.reshape(b * t, d), o_b.reshape(b * t, d), w["w_out"], w["post_mix_g"],
                   w["pre_ffn_g"], w["w_ff_up"], w["w_ff_down"], w["post_ffn_g"], tm=tm)
    return y.reshape(b, t, d), tail, s_new


def kernel(x_prompt, x_sample, state_conv, state_shift, state_wkv, pre_mix_g, post_mix_g, pre_ffn_g, post_ffn_g, w_in, conv_w, conv_b, conv_ln_g, conv_ln_b, w_conv_out, shift_mu, decay_base, w_decay_up, iclr_base, w_iclr_up, w_gate_up, k_k, k_a, r_k, lnx_g, lnx_b, w_rwkv_out, w_out, w_ff_up, w_ff_down):
    depth = w_in.shape[0]
    assert depth == 1
    d = x_prompt.shape[-1]
    n_heads = state_wkv.shape[2]
    r_dec, r_iclr, r_gate = w_decay_up.shape[1], w_iclr_up.shape[1], w_gate_up.shape[1]
    assert n_heads * HEAD_N == d and r_dec + r_iclr == LANES and r_gate == LANES
    n_lora = r_dec + r_iclr + r_gate
    i0 = 2 * d
    i1 = i0 + 3 * d + n_lora
    row = lambda a: a.reshape(1, -1)

    w_in_l = w_in[0]
    w_in_p = jnp.concatenate([w_in_l[:, :i0], w_in_l[:, i1:], w_in_l[:, i0:i1]], axis=1).astype(BF16)
    zpad = lambda n: jnp.zeros((n, d), F32)
    w = dict(
        pre_mix_g=row(pre_mix_g[0]), post_mix_g=row(post_mix_g[0]), pre_ffn_g=row(pre_ffn_g[0]),
        post_ffn_g=row(post_ffn_g[0]), w_in=w_in_p,
        conv_w=conv_w[0], conv_b=row(conv_b[0]), conv_ln_g=row(conv_ln_g[0]), conv_ln_b=row(conv_ln_b[0]),
        w_conv_out=w_conv_out[0].astype(BF16), mu=row(shift_mu[0]),
        decay_base=row(decay_base[0]), iclr_base=row(iclr_base[0]),
        w_dec=jnp.concatenate([w_decay_up[0], zpad(r_iclr)], axis=0).astype(BF16),
        w_iclr=jnp.concatenate([zpad(r_dec), w_iclr_up[0]], axis=0).astype(BF16),
        w_gate=w_gate_up[0].astype(BF16),
        k_k=row(k_k[0]), k_a=row(k_a[0]), r_k=row(r_k[0]), lnx_g=row(lnx_g[0]), lnx_b=row(lnx_b[0]),
        w_rwkv_out=w_rwkv_out[0].astype(BF16), w_out=w_out[0].astype(BF16),
        w_ff_up=w_ff_up[0].astype(BF16), w_ff_down=w_ff_down[0].astype(BF16),
    )

    bp, tp, _ = x_prompt.shape
    yp, conv_p, s_p = _layer(
        x_prompt, jnp.zeros((bp, conv_w.shape[1] - 1, d), F32), jnp.zeros((bp, 1, 3 * d + n_lora), F32),
        jnp.zeros((bp, n_heads, HEAD_N, HEAD_N), F32), w, t_valid=tp, tt=256, tt_rwkv=512, tm=512, tm_in=512, nb=1)
    shift_p = _rms_rows(x_prompt[:, -1], w["pre_mix_g"])

    bs, ts, _ = x_sample.shape
    t_pad = 8
    xs = jnp.pad(x_sample, ((0, 0), (0, t_pad - ts), (0, 0)))
    prev0 = _norm_matmul(state_shift[0], w["pre_mix_g"], w_in_p[:, 4 * d:], tm=bs, tn=3 * d + n_lora,
                         apply_norm=False, name="shift_proj")
    ys, conv_s, s_s = _layer(
        xs, state_conv.reshape(state_conv.shape[1:]), prev0.reshape(bs, 1, -1),
        state_wkv.reshape(state_wkv.shape[1:]), w,
        t_valid=ts, tt=t_pad, tt_rwkv=t_pad, tm=256, tm_in=512, nb=8)
    shift_s = _rms_rows(x_sample[:, -1], w["pre_mix_g"])

    return (yp, ys[:, :ts], conv_p[None], shift_p[None], s_p[None],
            conv_s[None], shift_s[None], s_s[None])
```

```python
import functools
import math

import jax
import jax.numpy as jnp
from jax import lax
from jax.experimental import pallas as pl
from jax.experimental.pallas import tpu as pltpu

F32 = jnp.float32
BF16 = jnp.bfloat16

RMS_EPS = 1e-6
LN_EPS = 1e-5
GN_EPS = 64e-5
HEAD_N = 64
LANES = 128
SUBLANES = 8
MXU_N = 256
CHUNK = 64
CONV_HALO = 32
TOKEN_TILE = 512
CONV_ROW_BLOCK = 16
KK_NORM_FLOOR = 1e-24
VMEM_LIMIT = 52 * 1024 * 1024


def _cparams(n_axes):
    return pltpu.CompilerParams(dimension_semantics=("arbitrary",) * n_axes,
                                vmem_limit_bytes=VMEM_LIMIT)


def _rms(x, g):
    return x * lax.rsqrt(jnp.mean(x * x, axis=-1, keepdims=True) + RMS_EPS) * g


def _sigmoid(x):
    return 0.5 * jnp.tanh(0.5 * x) + 0.5


def _dot(a, b):
    return jnp.dot(a, b, preferred_element_type=F32)


def _dot_nt(a, b):
    return lax.dot_general(a, b, (((1,), (1,)), ((), ())), preferred_element_type=F32)


def _dot_tn(a, b):
    return lax.dot_general(a, b, (((0,), (0,)), ((), ())), preferred_element_type=F32)


def _proj_kernel(x_ref, g_ref, w_ref, o_ref, *, apply_norm, glu_width):
    x = x_ref[...]
    if apply_norm:
        x = _rms(x, g_ref[...])
    acc = _dot(x.astype(BF16), w_ref[...])
    if glu_width:
        glu = acc[:, 0:glu_width] * _sigmoid(acc[:, glu_width:2 * glu_width])
        o_ref[:, 0:glu_width] = jnp.where(pl.program_id(0) == 0, glu, acc[:, 0:glu_width])
        o_ref[:, glu_width:] = acc[:, glu_width:]
    else:
        o_ref[...] = acc


def _norm_matmul(x, g, w, *, tm, tn, apply_norm, name, glu_width=0):
    m, d = x.shape
    n = w.shape[1]
    assert m % tm == 0 and n % tn == 0 and 2 * glu_width <= tn
    return pl.pallas_call(
        functools.partial(_proj_kernel, apply_norm=apply_norm, glu_width=glu_width),
        grid=(n // tn, m // tm),
        in_specs=[pl.BlockSpec((tm, d), lambda j, i: (i, 0)),
                  pl.BlockSpec((1, d), lambda j, i: (0, 0)),
                  pl.BlockSpec((d, tn), lambda j, i: (0, j))],
        out_specs=pl.BlockSpec((tm, tn), lambda j, i: (i, j)),
        out_shape=jax.ShapeDtypeStruct((m, n), F32),
        compiler_params=_cparams(2),
        name=name,
    )(x, g, w)


def _rms_rows_kernel(x_ref, g_ref, o_ref):
    o_ref[...] = _rms(x_ref[...], g_ref[...])


def _rms_rows(x, g):
    return pl.pallas_call(_rms_rows_kernel, out_shape=jax.ShapeDtypeStruct(x.shape, F32),
                          name="shift_rows")(x, g)


def _conv_kernel(u_ref, st_ref, cw_ref, cb_ref, lg_ref, lb_ref, wo_ref,
                 oa_ref, tail_ref, ext_ref, carry_ref, c_ref, wb_ref, *, nb, tt, t_valid, rb, n_taps):
    halo = n_taps - 1
    lead = CONV_HALO - halo
    rows = CONV_HALO + tt
    d = cw_ref.shape[1]

    @pl.when(pl.program_id(1) == 0)
    def _():
        carry_ref[...] = st_ref[...]

    @pl.when((pl.program_id(0) == 0) & (pl.program_id(1) == 0))
    def _():
        for j in range(n_taps):
            wb_ref[j * SUBLANES:(j + 1) * SUBLANES, :] = jnp.broadcast_to(cw_ref[j:j + 1, :], (SUBLANES, d))

    for i in range(nb):
        ext_ref[0, 0:lead, :] = jnp.zeros((lead, d), F32)
        ext_ref[0, lead:CONV_HALO, :] = carry_ref[i]
        ext_ref[0, CONV_HALO:rows, :] = u_ref[i]
        for s in range(1, SUBLANES):
            ext_ref[s, 0:rows - SUBLANES, :] = ext_ref[0, s:s + rows - SUBLANES, :]
        for r0 in range(0, tt, rb):
            acc = jnp.broadcast_to(cb_ref[...], (rb, d))
            for j in range(n_taps):
                q, s = divmod(lead + j, SUBLANES)
                a = q * SUBLANES + r0
                w8 = wb_ref[j * SUBLANES:(j + 1) * SUBLANES, :]
                acc = acc + ext_ref[s, a:a + rb, :] * jnp.concatenate([w8] * (rb // SUBLANES), axis=0)
            c_ref[i * tt + r0:i * tt + r0 + rb, :] = acc
        tail = ext_ref[0, lead + t_valid:lead + t_valid + halo, :]
        carry_ref[i] = tail
        tail_ref[i] = tail
    c = c_ref[...]
    mu = jnp.mean(c, axis=-1, keepdims=True)
    cc = c - mu
    var = jnp.mean(cc * cc, axis=-1, keepdims=True)
    z = cc * lax.rsqrt(var + LN_EPS) * lg_ref[...] + lb_ref[...]
    z = z * _sigmoid(z)
    out = _dot(z.astype(BF16), wo_ref[...])
    for i in range(nb):
        oa_ref[i] = out[i * tt:(i + 1) * tt]


def _conv_branch(proj, conv_state, conv_w, conv_b, ln_g, ln_b, w_conv_out, *, nb, tt, t_valid):
    b, t, _ = proj.shape
    d = conv_w.shape[1]
    n_taps = conv_w.shape[0]
    assert b % nb == 0 and t % tt == 0 and (nb == 1 or t == tt)
    rb = min(tt, CONV_ROW_BLOCK)
    kern = functools.partial(_conv_kernel, nb=nb, tt=tt, t_valid=t_valid, rb=rb, n_taps=n_taps)
    row = lambda shape: pl.BlockSpec(shape, lambda i, j: (0, 0))
    return pl.pallas_call(
        kern,
        grid=(b // nb, t // tt),
        in_specs=[pl.BlockSpec((nb, tt, d), lambda i, j: (i, j, 0)),
                  pl.BlockSpec((nb, n_taps - 1, d), lambda i, j: (i, 0, 0)),
                  row((n_taps, d)), row((1, d)), row((1, d)), row((1, d)), row((d, d))],
        out_specs=[pl.BlockSpec((nb, tt, d), lambda i, j: (i, j, 0)),
                   pl.BlockSpec((nb, n_taps - 1, d), lambda i, j: (i, 0, 0))],
        out_shape=[jax.ShapeDtypeStruct((b, t, d), F32),
                   jax.ShapeDtypeStruct((b, n_taps - 1, d), F32)],
        scratch_shapes=[pltpu.VMEM((SUBLANES, CONV_HALO + tt, d), F32),
                        pltpu.VMEM((nb, n_taps - 1, d), F32),
                        pltpu.VMEM((nb * tt, d), F32),
                        pltpu.VMEM((n_taps * SUBLANES, d), F32)],
        compiler_params=_cparams(2),
        name="conv_branch",
    )(proj, conv_state, conv_w, conv_b, ln_g, ln_b, w_conv_out)


def _split3(x):
    hi = x.astype(BF16)
    r1 = x - hi.astype(F32)
    mid = r1.astype(BF16)
    lo = (r1 - mid.astype(F32)).astype(BF16)
    return hi, mid, lo


def _rwkv_kernel(r_ref, k_ref, v_ref, lo_ref, prev_ref, s0_ref,
                 mu_ref, dbase_ref, wdec_ref, ibase_ref, wiclr_ref, wgate_ref,
                 kkw_ref, ka_ref, rk_ref, lng_ref, lnb_ref, wout_ref,
                 ob_ref, sout_ref,
                 sh_ref, at_b, rt_b, bh_b, kh_b, be_b, ke_b, v_b, dall_s, g_s, bo_s,
                 tm_s, arb_s, lv_s, yv_s, y_s, sbd_ref,
                 *, nb, tt, t_valid, n_sq):
    d = r_ref.shape[2]
    n_lora = lo_ref.shape[2]
    n_pairs = d // LANES
    rows_t = nb * tt
    top = SUBLANES
    seg = min(tt, CHUNK)
    sub = min(rows_t, 2 * CHUNK)
    ti = pl.program_id(1)

    @pl.when(ti == 0)
    def _():
        zh = jnp.zeros((HEAD_N, HEAD_N), F32)
        for i in range(nb):
            for p in range(n_pairs):
                sbd_ref[i, p] = jnp.concatenate(
                    [jnp.concatenate([s0_ref[i, 2 * p], zh], axis=1),
                     jnp.concatenate([zh, s0_ref[i, 2 * p + 1]], axis=1)], axis=0)

    for i in range(nb):
        r0 = top + i * tt
        sh_ref[r0:r0 + tt, 0:d] = r_ref[i]
        sh_ref[r0:r0 + tt, d:2 * d] = k_ref[i]
        sh_ref[r0:r0 + tt, 2 * d:3 * d] = v_ref[i]
        sh_ref[r0:r0 + tt, 3 * d:3 * d + n_lora] = lo_ref[i]

    @pl.when(ti == 0)
    def _():
        for i in range(nb):
            sh_ref[top - 1 + i * tt:top + i * tt, :] = prev_ref[i]

    li = lax.broadcasted_iota(jnp.int32, (LANES, LANES), 0) // HEAD_N
    lj = lax.broadcasted_iota(jnp.int32, (LANES, LANES), 1) // HEAD_N
    same_head = li == lj
    wi = lax.broadcasted_iota(jnp.int32, (MXU_N, MXU_N), 0) // HEAD_N
    wj = lax.broadcasted_iota(jnp.int32, (MXU_N, MXU_N), 1) // HEAD_N
    ones_bd = jnp.where(wi == wj, 1.0, 0.0).astype(BF16)

    def head_sum(x):
        parts = [_dot(x[:, c0:c0 + MXU_N].astype(BF16), ones_bd) for c0 in range(0, d, MXU_N)]
        return jnp.concatenate(parts, axis=1)

    ri = lax.broadcasted_iota(jnp.int32, (sub, sub), 0)
    rj = lax.broadcasted_iota(jnp.int32, (sub, sub), 1)
    same_seg = (ri // seg) == (rj // seg)
    tril_blk = jnp.where(rj <= ri, jnp.where(same_seg, 1.0, 0.0), 0.0).astype(BF16)

    def prep(q0):
        def mixed(c0, c1):
            cur = sh_ref[top + q0:top + q0 + sub, c0:c1]
            prv = sh_ref[top - 1 + q0:top - 1 + q0 + sub, c0:c1]
            return cur + (prv - cur) * mu_ref[:, c0:c1]

        lora = mixed(3 * d, 3 * d + n_lora)
        l_di = lora[:, 0:LANES]
        l_g = lora[:, LANES:2 * LANES]
        dec_in = dbase_ref[...] + _dot(jnp.tanh(l_di).astype(BF16), wdec_ref[...])
        a = _sigmoid(ibase_ref[...] + _dot(l_di.astype(BF16), wiclr_ref[...]))
        g_s[q0:q0 + sub, :] = _dot(_sigmoid(l_g).astype(BF16), wgate_ref[...])
        yield
        z = -dec_in
        softplus = jnp.maximum(z, 0.0) + jnp.log(1.0 + jnp.exp(-jnp.abs(z)))
        lw = -jnp.exp(-softplus - 0.5)
        live = None
        if t_valid < tt:
            live = (lax.broadcasted_iota(jnp.int32, (sub, d), 0) + q0) % tt < t_valid
            lw = jnp.where(live, lw, 0.0)
        hi, mid, lo = _split3(lw)
        cum = _dot(tril_blk, hi) + _dot(tril_blk, mid) + _dot(tril_blk, lo)
        tot = jnp.concatenate([jnp.broadcast_to(cum[e - 1:e, :], (seg, d)) for e in range(seg, sub + 1, seg)],
                              axis=0)
        dec_all = jnp.exp(tot)
        for sg in range(sub // seg):
            dall_s[q0 // seg + sg:q0 // seg + sg + 1, :] = dec_all[sg * seg:sg * seg + 1, :]
        yield

        def put(ref, val, mask=True):
            if mask and live is not None:
                val = jnp.where(live, val, 0.0)
            ref[q0:q0 + sub, :] = val.astype(ref.dtype)

        k = mixed(d, 2 * d)
        kk = k * kkw_ref[...]
        kk = kk * lax.rsqrt(jnp.maximum(head_sum(kk * kk), KK_NORM_FLOOR))
        yield
        put(at_b, -kk * jnp.exp(cum - lw))
        beta = kk * a
        put(bh_b, beta * jnp.exp(-cum))
        yield
        put(be_b, beta * jnp.exp(tot - cum))
        kt = k * (1.0 + (a - 1.0) * ka_ref[...])
        yield
        put(kh_b, kt * jnp.exp(-cum))
        put(ke_b, kt * jnp.exp(tot - cum))
        yield
        r = mixed(0, d)
        v = mixed(2 * d, 3 * d)
        bo_s[q0:q0 + sub, :] = head_sum(r * kt * rk_ref[...]) * v
        yield
        put(rt_b, r * jnp.exp(cum), mask=False)
        put(v_b, v)
        yield

    pt = lax.broadcasted_iota(jnp.int32, (CHUNK, LANES), 0)
    pl_ = lax.broadcasted_iota(jnp.int32, (CHUNK, LANES), 1)
    ps = pl_ % HEAD_N
    strict = ps < pt
    incl = ps <= pt
    if seg < CHUNK:
        own = (ps // seg) == (pt // seg)
        strict = jnp.logical_and(own, strict)
        incl = jnp.logical_and(own, incl)
    lane_lo = pl_ < HEAD_N
    eye_pair = jnp.where(ps == pt, 1.0, 0.0)
    zc = jnp.zeros((CHUNK, LANES), F32)

    in_lo = jnp.where(lane_lo, 1.0, 0.0).astype(BF16) > 0
    zb = jnp.zeros((CHUNK, LANES), BF16)

    def bd(xb):
        return jnp.concatenate([jnp.where(in_lo, xb, zb), jnp.where(in_lo, zb, xb)], axis=0)

    def blk(ref, c, p):
        return ref[c * CHUNK:(c + 1) * CHUNK, p * LANES:(p + 1) * LANES]

    def blkb(ref, c, p):
        return blk(ref, c, p).astype(BF16)

    def put_blk(ref, c, p, val):
        ref[c * CHUNK:(c + 1) * CHUNK, p * LANES:(p + 1) * LANES] = val

    def chunk_setup(its):
        ar = [_dot_nt(jnp.concatenate([blkb(at_b, c, p), blkb(rt_b, c, p)], axis=0),
                      jnp.concatenate([bd(blkb(bh_b, c, p)), bd(blkb(kh_b, c, p))], axis=0))
              for c, p in its]
        yield
        l_ab = [jnp.where(strict, a[0:CHUNK, 0:LANES], zc) for a in ar]
        for (c, p), a in zip(its, ar):
            put_blk(arb_s, c, p, jnp.where(incl, a[CHUNK:2 * CHUNK, 0:LANES], zc).astype(BF16))
        lv_yv = [_dot(jnp.concatenate([jnp.where(strict, a[0:CHUNK, LANES:2 * LANES], zc),
                                       jnp.where(incl, a[CHUNK:2 * CHUNK, LANES:2 * LANES], zc)],
                                      axis=0).astype(BF16), bd(blkb(v_b, c, p)))
                 for (c, p), a in zip(its, ar)]
        yield
        for (c, p), m in zip(its, lv_yv):
            put_blk(lv_s, c, p, m[0:CHUNK])
            put_blk(yv_s, c, p, m[CHUNK:2 * CHUNK])
        tm = [eye_pair + l for l in l_ab]
        if n_sq > 0:
            pw = [_dot(l.astype(BF16), bd(l.astype(BF16))) for l in l_ab]
            yield
        for i in range(n_sq):
            if i == n_sq - 1:
                tm = [t + _dot(q.astype(BF16), bd(t.astype(BF16))) for t, q in zip(tm, pw)]
            else:
                pr = [_dot(q.astype(BF16), jnp.concatenate([bd(q.astype(BF16)), bd(t.astype(BF16))], axis=1))
                      for t, q in zip(tm, pw)]
                pw = [x[:, 0:LANES] for x in pr]
                tm = [t + x[:, LANES:2 * LANES] for t, x in zip(tm, pr)]
            yield
        for (c, p), t in zip(its, tm):
            put_blk(tm_s, c, p, t.astype(BF16))
        yield

    pairs = range(n_pairs)
    segs = range(CHUNK // seg)

    def cat_rows(parts):
        return parts[0] if len(parts) == 1 else jnp.concatenate(parts, axis=0)

    def stack_bf16(top, bot):
        if seg % 16 == 0:
            return jnp.concatenate([top.astype(BF16), bot.astype(BF16)], axis=0)
        return jnp.concatenate([top.astype(F32), bot.astype(F32)], axis=0).astype(BF16)

    def chunk_step(c):
        def seq_of(sg):
            return (c * CHUNK + sg * seg) // tt if nb > 1 else 0

        def seg_rows(ref, sg, p):
            r0 = c * CHUNK + sg * seg
            return ref[r0:r0 + seg, p * LANES:(p + 1) * LANES]

        s_old = [[sbd_ref[seq_of(sg), p] for p in pairs] for sg in segs]
        x = [[_dot_nt(stack_bf16(seg_rows(at_b, sg, p), seg_rows(rt_b, sg, p)), s_old[sg][p].astype(BF16))
              for p in pairs] for sg in segs]
        yield
        x_a = [cat_rows([x[sg][p][0:seg] for sg in segs]) for p in pairs]
        x_r = [cat_rows([x[sg][p][seg:2 * seg] for sg in segs]) for p in pairs]
        u = [_dot(blk(tm_s, c, p), bd((x_a[p] + blk(lv_s, c, p)).astype(BF16))) for p in pairs]
        yield
        ub = [t.astype(BF16) for t in u]
        for p in pairs:
            put_blk(y_s, c, p, x_r[p] + blk(yv_s, c, p) + _dot(blk(arb_s, c, p), bd(ub[p])))
        yield
        for sg in segs:
            upd = [_dot_tn(stack_bf16(u[p][sg * seg:(sg + 1) * seg], seg_rows(v_b, sg, p)),
                           stack_bf16(seg_rows(be_b, sg, p), seg_rows(ke_b, sg, p))) for p in pairs]
            gi = c * (CHUNK // seg) + sg
            for p in pairs:
                sbd_ref[seq_of(sg), p] = (s_old[sg][p] * dall_s[gi:gi + 1, p * LANES:(p + 1) * LANES]
                                          + jnp.where(same_head, upd[p], 0.0))
            yield

    def finish(q0):
        y_n = []
        for c0 in range(0, d, MXU_N):
            yb = y_s[q0:q0 + sub, c0:c0 + MXU_N]
            yc = yb - _dot(yb.astype(BF16), ones_bd) * (1.0 / HEAD_N)
            var = _dot((yc * yc).astype(BF16), ones_bd) * (1.0 / HEAD_N)
            y_n.append(yc * lax.rsqrt(var + GN_EPS))
            yield
        y_n = jnp.concatenate(y_n, axis=1) * lng_ref[...] + lnb_ref[...]
        out = _dot(((y_n + bo_s[q0:q0 + sub, :]) * g_s[q0:q0 + sub, :]).astype(BF16), wout_ref[...])
        if nb == 1:
            ob_ref[0, q0:q0 + sub, :] = out
        else:
            for i in range(sub // tt):
                ob_ref[q0 // tt + i] = out[i * tt:(i + 1) * tt]
        yield

    cps = sub // CHUNK
    n_sub = rows_t // sub

    def chunk_walk(s):
        for c in range(cps):
            yield from chunk_step(s * cps + c)

    def round_robin(gens):
        while gens:
            for gen in list(gens):
                if next(gen, StopIteration) is StopIteration:
                    gens.remove(gen)

    for slot in range(n_sub + 3):
        live_gens = []
        if slot < n_sub:
            live_gens.append(prep(slot * sub))
        if 0 <= slot - 1 < n_sub:
            live_gens.append(chunk_setup([((slot - 1) * cps + c, p) for c in range(cps) for p in pairs]))
        if 0 <= slot - 2 < n_sub:
            live_gens.append(chunk_walk(slot - 2))
        if 0 <= slot - 3 < n_sub:
            live_gens.append(finish((slot - 3) * sub))
        round_robin(live_gens)
        if slot == 0 and nb == 1:
            sh_ref[top - 1:top, :] = sh_ref[top - 1 + t_valid:top + t_valid, :]

    @pl.when(ti == pl.num_programs(1) - 1)
    def _():
        for i in range(nb):
            for p in range(n_pairs):
                sout_ref[i, 2 * p] = sbd_ref[i, p, 0:HEAD_N, 0:HEAD_N]
                sout_ref[i, 2 * p + 1] = sbd_ref[i, p, HEAD_N:2 * HEAD_N, HEAD_N:2 * HEAD_N]


def _rwkv_branch(proj, prev0, s0, mu, dbase, wdec, ibase, wiclr, wgate, kkw, ka, rk, lng, lnb, wout,
                 *, nb, tt, t_valid):
    b, t, _ = proj.shape
    d = wout.shape[0]
    n_lora = wgate.shape[0] * 2
    n_pairs = d // LANES
    rows_s = nb * tt
    seg = min(tt, CHUNK)
    assert b % nb == 0 and t % tt == 0 and rows_s % CHUNK == 0 and max(tt, CHUNK) % seg == 0
    assert nb == 1 or (t == tt and t_valid < tt)
    n_sq = max(0, math.ceil(math.log2(min(seg, t_valid))) - 1)
    kern = functools.partial(_rwkv_kernel, nb=nb, tt=tt, t_valid=t_valid, n_sq=n_sq)
    const = lambda shape: pl.BlockSpec(shape, lambda i, j: (0,) * len(shape))
    col = lambda c: pl.BlockSpec((nb, tt, d), lambda i, j: (i, j, c))
    big = lambda dt: pltpu.VMEM((rows_s, d), dt)
    op_dt = BF16 if seg % 16 == 0 else F32
    return pl.pallas_call(
        kern,
        grid=(b // nb, t // tt),
        in_specs=[col(4), col(5), col(6),
                  pl.BlockSpec((nb, tt, n_lora), lambda i, j: (i, j, 7 * d // n_lora)),
                  pl.BlockSpec((nb, 1, 3 * d + n_lora), lambda i, j: (i, 0, 0)),
                  pl.BlockSpec((nb, 2 * n_pairs, HEAD_N, HEAD_N), lambda i, j: (i, 0, 0, 0)),
                  const((1, 3 * d + n_lora)), const((1, d)), const((LANES, d)), const((1, d)),
                  const((LANES, d)), const((LANES, d)), const((1, d)), const((1, d)), const((1, d)),
                  const((1, d)), const((1, d)), const((d, d))],
        out_specs=[pl.BlockSpec((nb, tt, d), lambda i, j: (i, j, 0)),
                   pl.BlockSpec((nb, 2 * n_pairs, HEAD_N, HEAD_N), lambda i, j: (i, 0, 0, 0))],
        out_shape=[jax.ShapeDtypeStruct((b, t, d), F32),
                   jax.ShapeDtypeStruct((b, 2 * n_pairs, HEAD_N, HEAD_N), F32)],
        scratch_shapes=[pltpu.VMEM((SUBLANES + rows_s, 3 * d + n_lora), F32),
                        big(op_dt), big(op_dt), big(BF16), big(BF16), big(op_dt), big(op_dt), big(op_dt),
                        pltpu.VMEM((max(rows_s // seg, SUBLANES), d), F32),
                        big(F32), big(F32),
                        big(BF16), big(BF16), big(F32), big(F32), big(F32),
                        pltpu.VMEM((nb, n_pairs, LANES, LANES), F32)],
        compiler_params=_cparams(2),
        name="rwkv_branch",
    )(proj, proj, proj, proj, prev0, s0, mu, dbase, wdec, ibase, wiclr, wgate, kkw, ka, rk, lng, lnb, wout)


def _ffn_kernel(x_ref, ga_ref, gb_ref, oa_ref, ob_ref, wo_ref, g1_ref, g2_ref, wu_ref, wd_ref, g3_ref,
                o_ref, *, ff_chunk):
    merged = _sigmoid(ga_ref[...]) * oa_ref[...] + _sigmoid(gb_ref[...]) * ob_ref[...]
    m = _dot(merged.astype(BF16), wo_ref[...])
    x1 = x_ref[...] + _rms(m, g1_ref[...])
    h2 = _rms(x1, g2_ref[...]).astype(BF16)
    f = None
    for c0 in range(0, wu_ref.shape[1], ff_chunk):
        up = jnp.maximum(_dot(h2, wu_ref[:, c0:c0 + ff_chunk]), 0.0)
        part = _dot((up * up).astype(BF16), wd_ref[c0:c0 + ff_chunk, :])
        f = part if f is None else f + part
    o_ref[...] = x1 + _rms(f, g3_ref[...])


def _merge_ffn(x, proj, o_a, o_b, w_out, g1, g2, w_up, w_down, g3, *, tm):
    m, d = x.shape
    dff = w_up.shape[1]
    tok = lambda c: pl.BlockSpec((tm, d), lambda i: (i, c))
    const = lambda shape: pl.BlockSpec(shape, lambda i: (0, 0), pipeline_mode=pl.Buffered(1))
    return pl.pallas_call(
        functools.partial(_ffn_kernel, ff_chunk=d),
        grid=(m // tm,),
        in_specs=[tok(0), tok(2), tok(3), tok(0), tok(0),
                  const((d, d)), const((1, d)), const((1, d)), const((d, dff)), const((dff, d)),
                  const((1, d))],
        out_specs=tok(0),
        out_shape=jax.ShapeDtypeStruct((m, d), F32),
        compiler_params=_cparams(1),
        name="merge_ffn",
    )(x, proj, proj, o_a, o_b, w_out, g1, g2, w_up, w_down, g3)


def _layer(x, conv_state, prev0, s0, w, *, t_valid, tt, nb):
    b, t, d = x.shape
    x2 = x.reshape(b * t, d)
    tm = min(TOKEN_TILE, b * t)
    proj = _norm_matmul(x2, w["pre_mix_g"], w["w_in"], tm=tm, tn=w["w_in"].shape[1] // 2,
                        apply_norm=True, name="in_proj", glu_width=d)
    proj3 = proj.reshape(b, t, -1)
    o_b, s_new = _rwkv_branch(proj3, prev0, s0, w["mu"], w["decay_base"], w["w_dec"], w["iclr_base"],
                              w["w_iclr"], w["w_gate"], w["k_k"], w["k_a"], w["r_k"], w["lnx_g"],
                              w["lnx_b"], w["w_rwkv_out"], nb=nb, tt=tt, t_valid=min(t_valid, tt))
    o_a, tail = _conv_branch(proj3, conv_state, w["conv_w"], w["conv_b"], w["conv_ln_g"], w["conv_ln_b"],
                             w["w_conv_out"], nb=nb, tt=tt, t_valid=min(t_valid, tt))
    y = _merge_ffn(x2, proj, o_a.reshape(b * t, d), o_b.reshape(b * t, d), w["w_out"], w["post_mix_g"],
                   w["pre_ffn_g"], w["w_ff_up"], w["w_ff_down"], w["post_ffn_g"], tm=tm)
    return y.reshape(b, t, d), tail, s_new


def kernel(x_prompt, x_sample, state_conv, state_shift, state_wkv, pre_mix_g, post_mix_g, pre_ffn_g, post_ffn_g, w_in, conv_w, conv_b, conv_ln_g, conv_ln_b, w_conv_out, shift_mu, decay_base, w_decay_up, iclr_base, w_iclr_up, w_gate_up, k_k, k_a, r_k, lnx_g, lnx_b, w_rwkv_out, w_out, w_ff_up, w_ff_down):
    depth = w_in.shape[0]
    assert depth == 1
    d = x_prompt.shape[-1]
    n_heads = state_wkv.shape[2]
    r_dec, r_iclr, r_gate = w_decay_up.shape[1], w_iclr_up.shape[1], w_gate_up.shape[1]
    assert n_heads * HEAD_N == d and r_dec + r_iclr == LANES and r_gate == LANES
    n_lora = r_dec + r_iclr + r_gate
    i0 = 2 * d
    i1 = i0 + 3 * d + n_lora
    row = lambda a: a.reshape(1, -1)

    w_in_l = w_in[0]
    w_in_p = jnp.concatenate([w_in_l[:, :i0], w_in_l[:, i1:], w_in_l[:, i0:i1]], axis=1).astype(BF16)
    zpad = lambda n: jnp.zeros((n, d), F32)
    w = dict(
        pre_mix_g=row(pre_mix_g[0]), post_mix_g=row(post_mix_g[0]), pre_ffn_g=row(pre_ffn_g[0]),
        post_ffn_g=row(post_ffn_g[0]), w_in=w_in_p,
        conv_w=conv_w[0], conv_b=row(conv_b[0]), conv_ln_g=row(conv_ln_g[0]), conv_ln_b=row(conv_ln_b[0]),
        w_conv_out=w_conv_out[0].astype(BF16), mu=row(shift_mu[0]),
        decay_base=row(decay_base[0]), iclr_base=row(iclr_base[0]),
        w_dec=jnp.concatenate([w_decay_up[0], zpad(r_iclr)], axis=0).astype(BF16),
        w_iclr=jnp.concatenate([zpad(r_dec), w_iclr_up[0]], axis=0).astype(BF16),
        w_gate=w_gate_up[0].astype(BF16),
        k_k=row(k_k[0]), k_a=row(k_a[0]), r_k=row(r_k[0]), lnx_g=row(lnx_g[0]), lnx_b=row(lnx_b[0]),
        w_rwkv_out=w_rwkv_out[0].astype(BF16), w_out=w_out[0].astype(BF16),
        w_ff_up=w_ff_up[0].astype(BF16), w_ff_down=w_ff_down[0].astype(BF16),
    )

    bp, tp, _ = x_prompt.shape
    yp, conv_p, s_p = _layer(
        x_prompt, jnp.zeros((bp, conv_w.shape[1] - 1, d), F32), jnp.zeros((bp, 1, 3 * d + n_lora), F32),
        jnp.zeros((bp, n_heads, HEAD_N, HEAD_N), F32), w, t_valid=tp, tt=TOKEN_TILE, nb=1)
    shift_p = _rms_rows(x_prompt[:, -1], w["pre_mix_g"])

    bs, ts, _ = x_sample.shape
    t_pad = SUBLANES
    assert ts < t_pad
    xs = jnp.pad(x_sample, ((0, 0), (0, t_pad - ts), (0, 0)))
    prev0 = _norm_matmul(state_shift[0], w["pre_mix_g"], w_in_p[:, 4 * d:], tm=bs, tn=3 * d + n_lora,
                         apply_norm=False, name="shift_proj")
    ys, conv_s, s_s = _layer(
        xs, state_conv.reshape(state_conv.shape[1:]), prev0.reshape(bs, 1, -1),
        state_wkv.reshape(state_wkv.shape[1:]), w,
        t_valid=ts, tt=t_pad, nb=CHUNK // t_pad)
    shift_s = _rms_rows(x_sample[:, -1], w["pre_mix_g"])

    return (yp, ys[:, :ts], conv_p[None], shift_p[None], s_p[None],
            conv_s[None], shift_s[None], s_s[None])
```

```python
import functools
import math

import jax
import jax.numpy as jnp
from jax import lax
from jax.experimental import pallas as pl
from jax.experimental.pallas import tpu as pltpu

F32 = jnp.float32
BF16 = jnp.bfloat16

RMS_EPS = 1e-6
LN_EPS = 1e-5
GN_EPS = 64e-5
HEAD_N = 64
LANES = 128
SUBLANES = 8
MXU_N = 256
CHUNK = 64
CONV_HALO = 32
TOKEN_TILE = 512
CONV_ROW_BLOCK = 16
KK_NORM_FLOOR = 1e-24
VMEM_LIMIT = 52 * 1024 * 1024


def _cparams(n_axes):
    return pltpu.CompilerParams(dimension_semantics=("arbitrary",) * n_axes,
                                vmem_limit_bytes=VMEM_LIMIT)


def _rms(x, g):
    return x * lax.rsqrt(jnp.mean(x * x, axis=-1, keepdims=True) + RMS_EPS) * g


def _sigmoid(x):
    return 0.5 * jnp.tanh(0.5 * x) + 0.5


def _dot(a, b):
    return jnp.dot(a, b, preferred_element_type=F32)


def _dot_nt(a, b):
    return lax.dot_general(a, b, (((1,), (1,)), ((), ())), preferred_element_type=F32)


def _dot_tn(a, b):
    return lax.dot_general(a, b, (((0,), (0,)), ((), ())), preferred_element_type=F32)


def _proj_kernel(x_ref, g_ref, w_ref, o_ref, *, apply_norm, glu_width):
    x = x_ref[...]
    if apply_norm:
        x = _rms(x, g_ref[...])
    acc = _dot(x.astype(BF16), w_ref[...])
    if glu_width:
        glu = acc[:, 0:glu_width] * _sigmoid(acc[:, glu_width:2 * glu_width])
        o_ref[:, 0:glu_width] = jnp.where(pl.program_id(0) == 0, glu, acc[:, 0:glu_width])
        o_ref[:, glu_width:] = acc[:, glu_width:]
    else:
        o_ref[...] = acc


def _norm_matmul(x, g, w, *, tm, tn, apply_norm, name, glu_width=0):
    m, d = x.shape
    n = w.shape[1]
    assert m % tm == 0 and n % tn == 0 and 2 * glu_width <= tn
    return pl.pallas_call(
        functools.partial(_proj_kernel, apply_norm=apply_norm, glu_width=glu_width),
        grid=(n // tn, m // tm),
        in_specs=[pl.BlockSpec((tm, d), lambda j, i: (i, 0)),
                  pl.BlockSpec((1, d), lambda j, i: (0, 0)),
                  pl.BlockSpec((d, tn), lambda j, i: (0, j))],
        out_specs=pl.BlockSpec((tm, tn), lambda j, i: (i, j)),
        out_shape=jax.ShapeDtypeStruct((m, n), F32),
        compiler_params=_cparams(2),
        name=name,
    )(x, g, w)


def _rms_rows_kernel(x_ref, g_ref, o_ref):
    o_ref[...] = _rms(x_ref[...], g_ref[...])


def _rms_rows(x, g):
    return pl.pallas_call(_rms_rows_kernel, out_shape=jax.ShapeDtypeStruct(x.shape, F32),
                          name="shift_rows")(x, g)


def _conv_kernel(u_ref, st_ref, cw_ref, cb_ref, lg_ref, lb_ref, wo_ref,
                 oa_ref, tail_ref, ext_ref, carry_ref, c_ref, wb_ref, *, nb, tt, t_valid, rb, n_taps):
    halo = n_taps - 1
    lead = CONV_HALO - halo
    rows = CONV_HALO + tt
    d = cw_ref.shape[1]

    @pl.when(pl.program_id(1) == 0)
    def _():
        carry_ref[...] = st_ref[...]

    @pl.when((pl.program_id(0) == 0) & (pl.program_id(1) == 0))
    def _():
        for j in range(n_taps):
            wb_ref[j * SUBLANES:(j + 1) * SUBLANES, :] = jnp.broadcast_to(cw_ref[j:j + 1, :], (SUBLANES, d))

    for i in range(nb):
        ext_ref[0, 0:lead, :] = jnp.zeros((lead, d), F32)
        ext_ref[0, lead:CONV_HALO, :] = carry_ref[i]
        ext_ref[0, CONV_HALO:rows, :] = u_ref[i]
        for s in range(1, SUBLANES):
            ext_ref[s, 0:rows - SUBLANES, :] = ext_ref[0, s:s + rows - SUBLANES, :]
        for r0 in range(0, tt, rb):
            acc = jnp.broadcast_to(cb_ref[...], (rb, d))
            for j in range(n_taps):
                q, s = divmod(lead + j, SUBLANES)
                a = q * SUBLANES + r0
                w8 = wb_ref[j * SUBLANES:(j + 1) * SUBLANES, :]
                acc = acc + ext_ref[s, a:a + rb, :] * jnp.concatenate([w8] * (rb // SUBLANES), axis=0)
            c_ref[i * tt + r0:i * tt + r0 + rb, :] = acc
        tail = ext_ref[0, lead + t_valid:lead + t_valid + halo, :]
        carry_ref[i] = tail
        tail_ref[i] = tail
    c = c_ref[...]
    mu = jnp.mean(c, axis=-1, keepdims=True)
    cc = c - mu
    var = jnp.mean(cc * cc, axis=-1, keepdims=True)
    z = cc * lax.rsqrt(var + LN_EPS) * lg_ref[...] + lb_ref[...]
    z = z * _sigmoid(z)
    out = _dot(z.astype(BF16), wo_ref[...])
    for i in range(nb):
        oa_ref[i] = out[i * tt:(i + 1) * tt]


def _conv_branch(proj, conv_state, conv_w, conv_b, ln_g, ln_b, w_conv_out, *, nb, tt, t_valid):
    b, t, _ = proj.shape
    d = conv_w.shape[1]
    n_taps = conv_w.shape[0]
    assert b % nb == 0 and t % tt == 0 and (nb == 1 or t == tt)
    rb = min(tt, CONV_ROW_BLOCK)
    kern = functools.partial(_conv_kernel, nb=nb, tt=tt, t_valid=t_valid, rb=rb, n_taps=n_taps)
    row = lambda shape: pl.BlockSpec(shape, lambda i, j: (0, 0))
    return pl.pallas_call(
        kern,
        grid=(b // nb, t // tt),
        in_specs=[pl.BlockSpec((nb, tt, d), lambda i, j: (i, j, 0)),
                  pl.BlockSpec((nb, n_taps - 1, d), lambda i, j: (i, 0, 0)),
                  row((n_taps, d)), row((1, d)), row((1, d)), row((1, d)), row((d, d))],
        out_specs=[pl.BlockSpec((nb, tt, d), lambda i, j: (i, j, 0)),
                   pl.BlockSpec((nb, n_taps - 1, d), lambda i, j: (i, 0, 0))],
        out_shape=[jax.ShapeDtypeStruct((b, t, d), F32),
                   jax.ShapeDtypeStruct((b, n_taps - 1, d), F32)],
        scratch_shapes=[pltpu.VMEM((SUBLANES, CONV_HALO + tt, d), F32),
                        pltpu.VMEM((nb, n_taps - 1, d), F32),
                        pltpu.VMEM((nb * tt, d), F32),
                        pltpu.VMEM((n_taps * SUBLANES, d), F32)],
        compiler_params=_cparams(2),
        name="conv_branch",
    )(proj, conv_state, conv_w, conv_b, ln_g, ln_b, w_conv_out)


def _split3(x):
    hi = x.astype(BF16)
    r1 = x - hi.astype(F32)
    mid = r1.astype(BF16)
    lo = (r1 - mid.astype(F32)).astype(BF16)
    return hi, mid, lo


def _rwkv_kernel(r_ref, k_ref, v_ref, lo_ref, prev_ref, s0_ref,
                 mu_ref, dbase_ref, wdec_ref, ibase_ref, wiclr_ref, wgate_ref,
                 kkw_ref, ka_ref, rk_ref, lng_ref, lnb_ref, wout_ref,
                 ob_ref, sout_ref,
                 sh_ref, at_b, rt_b, bh_b, kh_b, be_b, ke_b, v_b, dall_s, g_s, bo_s,
                 tm_s, arb_s, lv_s, yv_s, y_s, sbd_ref,
                 *, nb, tt, t_valid, n_sq):
    d = r_ref.shape[2]
    n_lora = lo_ref.shape[2]
    n_pairs = d // LANES
    rows_t = nb * tt
    top = SUBLANES
    seg = min(tt, CHUNK)
    sub = min(rows_t, 2 * CHUNK)
    ti = pl.program_id(1)

    @pl.when(ti == 0)
    def _():
        zh = jnp.zeros((HEAD_N, HEAD_N), F32)
        for i in range(nb):
            for p in range(n_pairs):
                sbd_ref[i, p] = jnp.concatenate(
                    [jnp.concatenate([s0_ref[i, 2 * p], zh], axis=1),
                     jnp.concatenate([zh, s0_ref[i, 2 * p + 1]], axis=1)], axis=0)

    for i in range(nb):
        r0 = top + i * tt
        sh_ref[r0:r0 + tt, 0:d] = r_ref[i]
        sh_ref[r0:r0 + tt, d:2 * d] = k_ref[i]
        sh_ref[r0:r0 + tt, 2 * d:3 * d] = v_ref[i]
        sh_ref[r0:r0 + tt, 3 * d:3 * d + n_lora] = lo_ref[i]

    @pl.when(ti == 0)
    def _():
        for i in range(nb):
            sh_ref[top - 1 + i * tt:top + i * tt, :] = prev_ref[i]

    li = lax.broadcasted_iota(jnp.int32, (LANES, LANES), 0) // HEAD_N
    lj = lax.broadcasted_iota(jnp.int32, (LANES, LANES), 1) // HEAD_N
    same_head = li == lj
    wi = lax.broadcasted_iota(jnp.int32, (MXU_N, MXU_N), 0) // HEAD_N
    wj = lax.broadcasted_iota(jnp.int32, (MXU_N, MXU_N), 1) // HEAD_N
    ones_bd = jnp.where(wi == wj, 1.0, 0.0).astype(BF16)

    def head_sum(x):
        parts = [_dot(x[:, c0:c0 + MXU_N].astype(BF16), ones_bd) for c0 in range(0, d, MXU_N)]
        return jnp.concatenate(parts, axis=1)

    ri = lax.broadcasted_iota(jnp.int32, (sub, sub), 0)
    rj = lax.broadcasted_iota(jnp.int32, (sub, sub), 1)
    same_seg = (ri // seg) == (rj // seg)
    tril_blk = jnp.where(rj <= ri, jnp.where(same_seg, 1.0, 0.0), 0.0).astype(BF16)

    def prep(q0):
        def mixed(c0, c1):
            cur = sh_ref[top + q0:top + q0 + sub, c0:c1]
            prv = sh_ref[top - 1 + q0:top - 1 + q0 + sub, c0:c1]
            return cur + (prv - cur) * mu_ref[:, c0:c1]

        lora = mixed(3 * d, 3 * d + n_lora)
        l_di = lora[:, 0:LANES]
        l_g = lora[:, LANES:2 * LANES]
        dec_in = dbase_ref[...] + _dot(jnp.tanh(l_di).astype(BF16), wdec_ref[...])
        a = _sigmoid(ibase_ref[...] + _dot(l_di.astype(BF16), wiclr_ref[...]))
        g_s[q0:q0 + sub, :] = _dot(_sigmoid(l_g).astype(BF16), wgate_ref[...])
        yield
        z = -dec_in
        softplus = jnp.maximum(z, 0.0) + jnp.log(1.0 + jnp.exp(-jnp.abs(z)))
        lw = -jnp.exp(-softplus - 0.5)
        live = None
        if t_valid < tt:
            live = (lax.broadcasted_iota(jnp.int32, (sub, d), 0) + q0) % tt < t_valid
            lw = jnp.where(live, lw, 0.0)
        hi, mid, lo = _split3(lw)
        cum = _dot(tril_blk, hi) + _dot(tril_blk, mid) + _dot(tril_blk, lo)
        tot = jnp.concatenate([jnp.broadcast_to(cum[e - 1:e, :], (seg, d)) for e in range(seg, sub + 1, seg)],
                              axis=0)
        dec_all = jnp.exp(tot)
        for sg in range(sub // seg):
            dall_s[q0 // seg + sg:q0 // seg + sg + 1, :] = dec_all[sg * seg:sg * seg + 1, :]
        yield

        def put(ref, val, mask=True):
            if mask and live is not None:
                val = jnp.where(live, val, 0.0)
            ref[q0:q0 + sub, :] = val.astype(ref.dtype)

        k = mixed(d, 2 * d)
        kk = k * kkw_ref[...]
        kk = kk * lax.rsqrt(jnp.maximum(head_sum(kk * kk), KK_NORM_FLOOR))
        yield
        put(at_b, -kk * jnp.exp(cum - lw))
        beta = kk * a
        put(bh_b, beta * jnp.exp(-cum))
        yield
        put(be_b, beta * jnp.exp(tot - cum))
        kt = k * (1.0 + (a - 1.0) * ka_ref[...])
        yield
        put(kh_b, kt * jnp.exp(-cum))
        put(ke_b, kt * jnp.exp(tot - cum))
        yield
        r = mixed(0, d)
        v = mixed(2 * d, 3 * d)
        bo_s[q0:q0 + sub, :] = head_sum(r * kt * rk_ref[...]) * v
        yield
        put(rt_b, r * jnp.exp(cum), mask=False)
        put(v_b, v)
        yield

    pt = lax.broadcasted_iota(jnp.int32, (CHUNK, LANES), 0)
    pl_ = lax.broadcasted_iota(jnp.int32, (CHUNK, LANES), 1)
    ps = pl_ % HEAD_N
    strict = ps < pt
    incl = ps <= pt
    if seg < CHUNK:
        own = (ps // seg) == (pt // seg)
        strict = jnp.logical_and(own, strict)
        incl = jnp.logical_and(own, incl)
    lane_lo = pl_ < HEAD_N
    eye_pair = jnp.where(ps == pt, 1.0, 0.0)
    zc = jnp.zeros((CHUNK, LANES), F32)

    in_lo = jnp.where(lane_lo, 1.0, 0.0).astype(BF16) > 0
    zb = jnp.zeros((CHUNK, LANES), BF16)

    def bd(xb):
        return jnp.concatenate([jnp.where(in_lo, xb, zb), jnp.where(in_lo, zb, xb)], axis=0)

    def blk(ref, c, p):
        return ref[c * CHUNK:(c + 1) * CHUNK, p * LANES:(p + 1) * LANES]

    def blkb(ref, c, p):
        return blk(ref, c, p).astype(BF16)

    def put_blk(ref, c, p, val):
        ref[c * CHUNK:(c + 1) * CHUNK, p * LANES:(p + 1) * LANES] = val

    def chunk_setup(its):
        ar = [_dot_nt(jnp.concatenate([blkb(at_b, c, p), blkb(rt_b, c, p)], axis=0),
                      jnp.concatenate([bd(blkb(bh_b, c, p)), bd(blkb(kh_b, c, p))], axis=0))
              for c, p in its]
        yield
        l_ab = [jnp.where(strict, a[0:CHUNK, 0:LANES], zc) for a in ar]
        for (c, p), a in zip(its, ar):
            put_blk(arb_s, c, p, jnp.where(incl, a[CHUNK:2 * CHUNK, 0:LANES], zc).astype(BF16))
        lv_yv = [_dot(jnp.concatenate([jnp.where(strict, a[0:CHUNK, LANES:2 * LANES], zc),
                                       jnp.where(incl, a[CHUNK:2 * CHUNK, LANES:2 * LANES], zc)],
                                      axis=0).astype(BF16), bd(blkb(v_b, c, p)))
                 for (c, p), a in zip(its, ar)]
        yield
        for (c, p), m in zip(its, lv_yv):
            put_blk(lv_s, c, p, m[0:CHUNK])
            put_blk(yv_s, c, p, m[CHUNK:2 * CHUNK])
        tm = [eye_pair + l for l in l_ab]
        if n_sq > 0:
            pw = [_dot(l.astype(BF16), bd(l.astype(BF16))) for l in l_ab]
            yield
        for i in range(n_sq):
            if i == n_sq - 1:
                tm = [t + _dot(q.astype(BF16), bd(t.astype(BF16))) for t, q in zip(tm, pw)]
            else:
                pr = [_dot(q.astype(BF16), jnp.concatenate([bd(q.astype(BF16)), bd(t.astype(BF16))], axis=1))
                      for t, q in zip(tm, pw)]
                pw = [x[:, 0:LANES] for x in pr]
                tm = [t + x[:, LANES:2 * LANES] for t, x in zip(tm, pr)]
            yield
        for (c, p), t in zip(its, tm):
            put_blk(tm_s, c, p, t.astype(BF16))
        yield

    pairs = range(n_pairs)
    segs = range(CHUNK // seg)

    def cat_rows(parts):
        return parts[0] if len(parts) == 1 else jnp.concatenate(parts, axis=0)

    def stack_bf16(upper, lower):
        if seg % 16 == 0:
            return jnp.concatenate([upper.astype(BF16), lower.astype(BF16)], axis=0)
        return jnp.concatenate([upper.astype(F32), lower.astype(F32)], axis=0).astype(BF16)

    def chunk_step(c):
        def seq_of(sg):
            return (c * CHUNK + sg * seg) // tt if nb > 1 else 0

        def seg_rows(ref, sg, p):
            r0 = c * CHUNK + sg * seg
            return ref[r0:r0 + seg, p * LANES:(p + 1) * LANES]

        s_old = [[sbd_ref[seq_of(sg), p] for p in pairs] for sg in segs]
        x = [[_dot_nt(stack_bf16(seg_rows(at_b, sg, p), seg_rows(rt_b, sg, p)), s_old[sg][p].astype(BF16))
              for p in pairs] for sg in segs]
        yield
        x_a = [cat_rows([x[sg][p][0:seg] for sg in segs]) for p in pairs]
        x_r = [cat_rows([x[sg][p][seg:2 * seg] for sg in segs]) for p in pairs]
        u = [_dot(blk(tm_s, c, p), bd((x_a[p] + blk(lv_s, c, p)).astype(BF16))) for p in pairs]
        yield
        ub = [t.astype(BF16) for t in u]
        for p in pairs:
            put_blk(y_s, c, p, x_r[p] + blk(yv_s, c, p) + _dot(blk(arb_s, c, p), bd(ub[p])))
        yield
        for sg in segs:
            upd = [_dot_tn(stack_bf16(u[p][sg * seg:(sg + 1) * seg], seg_rows(v_b, sg, p)),
                           stack_bf16(seg_rows(be_b, sg, p), seg_rows(ke_b, sg, p))) for p in pairs]
            gi = c * (CHUNK // seg) + sg
            for p in pairs:
                sbd_ref[seq_of(sg), p] = (s_old[sg][p] * dall_s[gi:gi + 1, p * LANES:(p + 1) * LANES]
                                          + jnp.where(same_head, upd[p], 0.0))
            yield

    def finish(q0):
        y_n = []
        for c0 in range(0, d, MXU_N):
            yb = y_s[q0:q0 + sub, c0:c0 + MXU_N]
            yc = yb - _dot(yb.astype(BF16), ones_bd) * (1.0 / HEAD_N)
            var = _dot((yc * yc).astype(BF16), ones_bd) * (1.0 / HEAD_N)
            y_n.append(yc * lax.rsqrt(var + GN_EPS))
            yield
        y_n = jnp.concatenate(y_n, axis=1) * lng_ref[...] + lnb_ref[...]
        out = _dot(((y_n + bo_s[q0:q0 + sub, :]) * g_s[q0:q0 + sub, :]).astype(BF16), wout_ref[...])
        if nb == 1:
            ob_ref[0, q0:q0 + sub, :] = out
        else:
            for i in range(sub // tt):
                ob_ref[q0 // tt + i] = out[i * tt:(i + 1) * tt]
        yield

    cps = sub // CHUNK
    n_sub = rows_t // sub

    def chunk_walk(s):
        for c in range(cps):
            yield from chunk_step(s * cps + c)

    def round_robin(gens):
        while gens:
            for gen in list(gens):
                if next(gen, StopIteration) is StopIteration:
                    gens.remove(gen)

    for slot in range(n_sub + 3):
        live_gens = []
        if slot < n_sub:
            live_gens.append(prep(slot * sub))
        if 0 <= slot - 1 < n_sub:
            live_gens.append(chunk_setup([((slot - 1) * cps + c, p) for c in range(cps) for p in pairs]))
        if 0 <= slot - 2 < n_sub:
            live_gens.append(chunk_walk(slot - 2))
        if 0 <= slot - 3 < n_sub:
            live_gens.append(finish((slot - 3) * sub))
        round_robin(live_gens)
        if slot == 0 and nb == 1:
            sh_ref[top - 1:top, :] = sh_ref[top - 1 + t_valid:top + t_valid, :]

    @pl.when(ti == pl.num_programs(1) - 1)
    def _():
        for i in range(nb):
            for p in range(n_pairs):
                sout_ref[i, 2 * p] = sbd_ref[i, p, 0:HEAD_N, 0:HEAD_N]
                sout_ref[i, 2 * p + 1] = sbd_ref[i, p, HEAD_N:2 * HEAD_N, HEAD_N:2 * HEAD_N]


def _rwkv_branch(proj, prev0, s0, mu, dbase, wdec, ibase, wiclr, wgate, kkw, ka, rk, lng, lnb, wout,
                 *, nb, tt, t_valid):
    b, t, _ = proj.shape
    d = wout.shape[0]
    n_lora = wgate.shape[0] * 2
    n_pairs = d // LANES
    rows_s = nb * tt
    seg = min(tt, CHUNK)
    assert b % nb == 0 and t % tt == 0 and rows_s % CHUNK == 0 and max(tt, CHUNK) % seg == 0
    assert nb == 1 or (t == tt and t_valid < tt)
    n_sq = max(0, math.ceil(math.log2(min(seg, t_valid))) - 1)
    kern = functools.partial(_rwkv_kernel, nb=nb, tt=tt, t_valid=t_valid, n_sq=n_sq)
    const = lambda shape: pl.BlockSpec(shape, lambda i, j: (0,) * len(shape))
    col = lambda c: pl.BlockSpec((nb, tt, d), lambda i, j: (i, j, c))
    big = lambda dt: pltpu.VMEM((rows_s, d), dt)
    op_dt = BF16 if seg % 16 == 0 else F32
    return pl.pallas_call(
        kern,
        grid=(b // nb, t // tt),
        in_specs=[col(4), col(5), col(6),
                  pl.BlockSpec((nb, tt, n_lora), lambda i, j: (i, j, 7 * d // n_lora)),
                  pl.BlockSpec((nb, 1, 3 * d + n_lora), lambda i, j: (i, 0, 0)),
                  pl.BlockSpec((nb, 2 * n_pairs, HEAD_N, HEAD_N), lambda i, j: (i, 0, 0, 0)),
                  const((1, 3 * d + n_lora)), const((1, d)), const((LANES, d)), const((1, d)),
                  const((LANES, d)), const((LANES, d)), const((1, d)), const((1, d)), const((1, d)),
                  const((1, d)), const((1, d)), const((d, d))],
        out_specs=[pl.BlockSpec((nb, tt, d), lambda i, j: (i, j, 0)),
                   pl.BlockSpec((nb, 2 * n_pairs, HEAD_N, HEAD_N), lambda i, j: (i, 0, 0, 0))],
        out_shape=[jax.ShapeDtypeStruct((b, t, d), F32),
                   jax.ShapeDtypeStruct((b, 2 * n_pairs, HEAD_N, HEAD_N), F32)],
        scratch_shapes=[pltpu.VMEM((SUBLANES + rows_s, 3 * d + n_lora), F32),
                        big(op_dt), big(op_dt), big(BF16), big(BF16), big(op_dt), big(op_dt), big(op_dt),
                        pltpu.VMEM((max(rows_s // seg, SUBLANES), d), F32),
                        big(F32), big(F32),
                        big(BF16), big(BF16), big(F32), big(F32), big(F32),
                        pltpu.VMEM((nb, n_pairs, LANES, LANES), F32)],
        compiler_params=_cparams(2),
        name="rwkv_branch",
    )(proj, proj, proj, proj, prev0, s0, mu, dbase, wdec, ibase, wiclr, wgate, kkw, ka, rk, lng, lnb, wout)


def _ffn_kernel(x_ref, ga_ref, gb_ref, oa_ref, ob_ref, wo_ref, g1_ref, g2_ref, wu_ref, wd_ref, g3_ref,
                o_ref, *, ff_chunk):
    merged = _sigmoid(ga_ref[...]) * oa_ref[...] + _sigmoid(gb_ref[...]) * ob_ref[...]
    m = _dot(merged.astype(BF16), wo_ref[...])
    x1 = x_ref[...] + _rms(m, g1_ref[...])
    h2 = _rms(x1, g2_ref[...]).astype(BF16)
    f = None
    for c0 in range(0, wu_ref.shape[1], ff_chunk):
        up = jnp.maximum(_dot(h2, wu_ref[:, c0:c0 + ff_chunk]), 0.0)
        part = _dot((up * up).astype(BF16), wd_ref[c0:c0 + ff_chunk, :])
        f = part if f is None else f + part
    o_ref[...] = x1 + _rms(f, g3_ref[...])


def _merge_ffn(x, proj, o_a, o_b, w_out, g1, g2, w_up, w_down, g3, *, tm):
    m, d = x.shape
    dff = w_up.shape[1]
    tok = lambda c: pl.BlockSpec((tm, d), lambda i: (i, c))
    const = lambda shape: pl.BlockSpec(shape, lambda i: (0, 0), pipeline_mode=pl.Buffered(1))
    return pl.pallas_call(
        functools.partial(_ffn_kernel, ff_chunk=d),
        grid=(m // tm,),
        in_specs=[tok(0), tok(2), tok(3), tok(0), tok(0),
                  const((d, d)), const((1, d)), const((1, d)), const((d, dff)), const((dff, d)),
                  const((1, d))],
        out_specs=tok(0),
        out_shape=jax.ShapeDtypeStruct((m, d), F32),
        compiler_params=_cparams(1),
        name="merge_ffn",
    )(x, proj, proj, o_a, o_b, w_out, g1, g2, w_up, w_down, g3)


def _layer(x, conv_state, prev0, s0, w, *, t_valid, tt, nb):
    b, t, d = x.shape
    x2 = x.reshape(b * t, d)
    tm = min(TOKEN_TILE, b * t)
    proj = _norm_matmul(x2, w["pre_mix_g"], w["w_in"], tm=tm, tn=w["w_in"].shape[1] // 2,
                        apply_norm=True, name="in_proj", glu_width=d)
    proj3 = proj.reshape(b, t, -1)
    o_b, s_new = _rwkv_branch(proj3, prev0, s0, w["mu"], w["decay_base"], w["w_dec"], w["iclr_base"],
                              w["w_iclr"], w["w_gate"], w["k_k"], w["k_a"], w["r_k"], w["lnx_g"],
                              w["lnx_b"], w["w_rwkv_out"], nb=nb, tt=tt, t_valid=min(t_valid, tt))
    o_a, tail = _conv_branch(proj3, conv_state, w["conv_w"], w["conv_b"], w["conv_ln_g"], w["conv_ln_b"],
                             w["w_conv_out"], nb=nb, tt=tt, t_valid=min(t_valid, tt))
    y = _merge_ffn(x2, proj, o_a.reshape(b * t, d), o_b.reshape(b * t, d), w["w_out"], w["post_mix_g"],
                   w["pre_ffn_g"], w["w_ff_up"], w["w_ff_down"], w["post_ffn_g"], tm=tm)
    return y.reshape(b, t, d), tail, s_new


def kernel(x_prompt, x_sample, state_conv, state_shift, state_wkv, pre_mix_g, post_mix_g, pre_ffn_g, post_ffn_g, w_in, conv_w, conv_b, conv_ln_g, conv_ln_b, w_conv_out, shift_mu, decay_base, w_decay_up, iclr_base, w_iclr_up, w_gate_up, k_k, k_a, r_k, lnx_g, lnx_b, w_rwkv_out, w_out, w_ff_up, w_ff_down):
    depth = w_in.shape[0]
    assert depth == 1
    d = x_prompt.shape[-1]
    n_heads = state_wkv.shape[2]
    r_dec, r_iclr, r_gate = w_decay_up.shape[1], w_iclr_up.shape[1], w_gate_up.shape[1]
    assert n_heads * HEAD_N == d and r_dec + r_iclr == LANES and r_gate == LANES
    n_lora = r_dec + r_iclr + r_gate
    i0 = 2 * d
    i1 = i0 + 3 * d + n_lora
    row = lambda a: a.reshape(1, -1)

    w_in_l = w_in[0]
    w_in_p = jnp.concatenate([w_in_l[:, :i0], w_in_l[:, i1:], w_in_l[:, i0:i1]], axis=1).astype(BF16)
    zpad = lambda n: jnp.zeros((n, d), F32)
    w = dict(
        pre_mix_g=row(pre_mix_g[0]), post_mix_g=row(post_mix_g[0]), pre_ffn_g=row(pre_ffn_g[0]),
        post_ffn_g=row(post_ffn_g[0]), w_in=w_in_p,
        conv_w=conv_w[0], conv_b=row(conv_b[0]), conv_ln_g=row(conv_ln_g[0]), conv_ln_b=row(conv_ln_b[0]),
        w_conv_out=w_conv_out[0].astype(BF16), mu=row(shift_mu[0]),
        decay_base=row(decay_base[0]), iclr_base=row(iclr_base[0]),
        w_dec=jnp.concatenate([w_decay_up[0], zpad(r_iclr)], axis=0).astype(BF16),
        w_iclr=jnp.concatenate([zpad(r_dec), w_iclr_up[0]], axis=0).astype(BF16),
        w_gate=w_gate_up[0].astype(BF16),
        k_k=row(k_k[0]), k_a=row(k_a[0]), r_k=row(r_k[0]), lnx_g=row(lnx_g[0]), lnx_b=row(lnx_b[0]),
        w_rwkv_out=w_rwkv_out[0].astype(BF16), w_out=w_out[0].astype(BF16),
        w_ff_up=w_ff_up[0].astype(BF16), w_ff_down=w_ff_down[0].astype(BF16),
    )

    bp, tp, _ = x_prompt.shape
    yp, conv_p, s_p = _layer(
        x_prompt, jnp.zeros((bp, conv_w.shape[1] - 1, d), F32), jnp.zeros((bp, 1, 3 * d + n_lora), F32),
        jnp.zeros((bp, n_heads, HEAD_N, HEAD_N), F32), w, t_valid=tp, tt=TOKEN_TILE, nb=1)
    shift_p = _rms_rows(x_prompt[:, -1], w["pre_mix_g"])

    bs, ts, _ = x_sample.shape
    t_pad = SUBLANES
    assert ts < t_pad
    xs = jnp.pad(x_sample, ((0, 0), (0, t_pad - ts), (0, 0)))
    prev0 = _norm_matmul(state_shift[0], w["pre_mix_g"], w_in_p[:, 4 * d:], tm=bs, tn=3 * d + n_lora,
                         apply_norm=False, name="shift_proj")
    ys, conv_s, s_s = _layer(
        xs, state_conv.reshape(state_conv.shape[1:]), prev0.reshape(bs, 1, -1),
        state_wkv.reshape(state_wkv.shape[1:]), w,
        t_valid=ts, tt=t_pad, nb=CHUNK // t_pad)
    shift_s = _rms_rows(x_sample[:, -1], w["pre_mix_g"])

    return (yp, ys[:, :ts], conv_p[None], shift_p[None], s_p[None],
            conv_s[None], shift_s[None], s_s[None])
```

```python
import functools
import math

import jax
import jax.numpy as jnp
from jax import lax
from jax.experimental import pallas as pl
from jax.experimental.pallas import tpu as pltpu

F32 = jnp.float32
BF16 = jnp.bfloat16

RMS_EPS = 1e-6
LN_EPS = 1e-5
GN_EPS = 64e-5
HEAD_N = 64
LANES = 128
SUBLANES = 8
MXU_N = 256
CHUNK = 64
CONV_HALO = 32
TOKEN_TILE = 512
CONV_ROW_BLOCK = 16
KK_NORM_FLOOR = 1e-24
VMEM_LIMIT = 52 * 1024 * 1024


def _cparams(n_axes):
    return pltpu.CompilerParams(dimension_semantics=("arbitrary",) * n_axes,
                                vmem_limit_bytes=VMEM_LIMIT)


def _rms(x, g):
    return x * lax.rsqrt(jnp.mean(x * x, axis=-1, keepdims=True) + RMS_EPS) * g


def _sigmoid(x):
    return 0.5 * jnp.tanh(0.5 * x) + 0.5


def _dot(a, b):
    return jnp.dot(a, b, preferred_element_type=F32)


def _dot_nt(a, b):
    return lax.dot_general(a, b, (((1,), (1,)), ((), ())), preferred_element_type=F32)


def _dot_tn(a, b):
    return lax.dot_general(a, b, (((0,), (0,)), ((), ())), preferred_element_type=F32)


def _proj_kernel(x_ref, g_ref, w_ref, o_ref, *, apply_norm, glu_width):
    x = x_ref[...]
    if apply_norm:
        x = _rms(x, g_ref[...])
    acc = _dot(x.astype(BF16), w_ref[...])
    if glu_width:
        g = glu_width
        spill = 4 * g - o_ref.shape[1]
        first = pl.program_id(0) == 0
        glu = acc[:, 0:g] * _sigmoid(acc[:, g:2 * g])
        o_ref[:, 0:spill] = jnp.where(first, glu[:, 0:spill], _sigmoid(acc[:, 0:spill]))
        o_ref[:, spill:g] = jnp.where(first, glu[:, spill:g], acc[:, spill:g])
        o_ref[:, g:2 * g] = acc[:, g:2 * g]
        o_ref[:, 2 * g:] = jnp.where(first, _sigmoid(acc[:, 2 * g:]), acc[:, 2 * g:])
    else:
        o_ref[...] = acc


def _norm_matmul(x, g, w, *, tm, tn, apply_norm, name, glu_width=0):
    m, d = x.shape
    n = w.shape[1]
    assert m % tm == 0 and n % tn == 0
    assert glu_width == 0 or (n == 2 * tn and 2 * glu_width < tn < 4 * glu_width and 4 * glu_width - tn < glu_width)
    return pl.pallas_call(
        functools.partial(_proj_kernel, apply_norm=apply_norm, glu_width=glu_width),
        grid=(n // tn, m // tm),
        in_specs=[pl.BlockSpec((tm, d), lambda j, i: (i, 0)),
                  pl.BlockSpec((1, d), lambda j, i: (0, 0)),
                  pl.BlockSpec((d, tn), lambda j, i: (0, j))],
        out_specs=pl.BlockSpec((tm, tn), lambda j, i: (i, j)),
        out_shape=jax.ShapeDtypeStruct((m, n), F32),
        compiler_params=_cparams(2),
        name=name,
    )(x, g, w)


def _rms_rows_kernel(x_ref, g_ref, o_ref):
    o_ref[...] = _rms(x_ref[...], g_ref[...])


def _rms_rows(x, g):
    return pl.pallas_call(_rms_rows_kernel, out_shape=jax.ShapeDtypeStruct(x.shape, F32),
                          name="shift_rows")(x, g)


def _conv_kernel(u_ref, st_ref, cw_ref, cb_ref, lg_ref, lb_ref, wo_ref,
                 oa_ref, tail_ref, ext_ref, carry_ref, c_ref, wb_ref, *, nb, tt, t_valid, rb, n_taps):
    halo = n_taps - 1
    lead = CONV_HALO - halo
    rows = CONV_HALO + tt
    d = cw_ref.shape[1]

    @pl.when(pl.program_id(1) == 0)
    def _():
        carry_ref[...] = st_ref[...]

    @pl.when((pl.program_id(0) == 0) & (pl.program_id(1) == 0))
    def _():
        for j in range(n_taps):
            wb_ref[j * SUBLANES:(j + 1) * SUBLANES, :] = jnp.broadcast_to(cw_ref[j:j + 1, :], (SUBLANES, d))

    for i in range(nb):
        ext_ref[0, 0:lead, :] = jnp.zeros((lead, d), F32)
        ext_ref[0, lead:CONV_HALO, :] = carry_ref[i]
        ext_ref[0, CONV_HALO:rows, :] = u_ref[i]
        for s in range(1, SUBLANES):
            ext_ref[s, 0:rows - SUBLANES, :] = ext_ref[0, s:s + rows - SUBLANES, :]
        for r0 in range(0, tt, rb):
            acc = jnp.broadcast_to(cb_ref[...], (rb, d))
            for j in range(n_taps):
                q, s = divmod(lead + j, SUBLANES)
                a = q * SUBLANES + r0
                w8 = wb_ref[j * SUBLANES:(j + 1) * SUBLANES, :]
                acc = acc + ext_ref[s, a:a + rb, :] * jnp.concatenate([w8] * (rb // SUBLANES), axis=0)
            c_ref[i * tt + r0:i * tt + r0 + rb, :] = acc
        tail = ext_ref[0, lead + t_valid:lead + t_valid + halo, :]
        carry_ref[i] = tail
        tail_ref[i] = tail
    c = c_ref[...]
    mu = jnp.mean(c, axis=-1, keepdims=True)
    cc = c - mu
    var = jnp.mean(cc * cc, axis=-1, keepdims=True)
    z = cc * lax.rsqrt(var + LN_EPS) * lg_ref[...] + lb_ref[...]
    z = z * _sigmoid(z)
    out = _dot(z.astype(BF16), wo_ref[...])
    for i in range(nb):
        oa_ref[i] = out[i * tt:(i + 1) * tt]


def _conv_branch(proj, conv_state, conv_w, conv_b, ln_g, ln_b, w_conv_out, *, nb, tt, t_valid):
    b, t, _ = proj.shape
    d = conv_w.shape[1]
    n_taps = conv_w.shape[0]
    assert b % nb == 0 and t % tt == 0 and (nb == 1 or t == tt)
    rb = min(tt, CONV_ROW_BLOCK)
    kern = functools.partial(_conv_kernel, nb=nb, tt=tt, t_valid=t_valid, rb=rb, n_taps=n_taps)
    row = lambda shape: pl.BlockSpec(shape, lambda i, j: (0, 0))
    return pl.pallas_call(
        kern,
        grid=(b // nb, t // tt),
        in_specs=[pl.BlockSpec((nb, tt, d), lambda i, j: (i, j, 0)),
                  pl.BlockSpec((nb, n_taps - 1, d), lambda i, j: (i, 0, 0)),
                  row((n_taps, d)), row((1, d)), row((1, d)), row((1, d)), row((d, d))],
        out_specs=[pl.BlockSpec((nb, tt, d), lambda i, j: (i, j, 0)),
                   pl.BlockSpec((nb, n_taps - 1, d), lambda i, j: (i, 0, 0))],
        out_shape=[jax.ShapeDtypeStruct((b, t, d), F32),
                   jax.ShapeDtypeStruct((b, n_taps - 1, d), F32)],
        scratch_shapes=[pltpu.VMEM((SUBLANES, CONV_HALO + tt, d), F32),
                        pltpu.VMEM((nb, n_taps - 1, d), F32),
                        pltpu.VMEM((nb * tt, d), F32),
                        pltpu.VMEM((n_taps * SUBLANES, d), F32)],
        compiler_params=_cparams(2),
        name="conv_branch",
    )(proj, conv_state, conv_w, conv_b, ln_g, ln_b, w_conv_out)


def _split3(x):
    hi = x.astype(BF16)
    r1 = x - hi.astype(F32)
    mid = r1.astype(BF16)
    lo = (r1 - mid.astype(F32)).astype(BF16)
    return hi, mid, lo


def _rwkv_kernel(r_ref, k_ref, v_ref, lo_ref, prev_ref, s0_ref,
                 mu_ref, dbase_ref, wdec_ref, ibase_ref, wiclr_ref, wgate_ref,
                 kkw_ref, ka_ref, rk_ref, lng_ref, lnb_ref, wout_ref,
                 ob_ref, sout_ref,
                 sh_ref, at_b, rt_b, bh_b, kh_b, be_b, ke_b, v_b, dall_s, g_s, bo_s,
                 tm_s, arb_s, lv_s, yv_s, y_s, sbd_ref,
                 *, nb, tt, t_valid, n_sq):
    d = r_ref.shape[2]
    n_lora = lo_ref.shape[2]
    n_pairs = d // LANES
    rows_t = nb * tt
    top = SUBLANES
    seg = min(tt, CHUNK)
    sub = min(rows_t, 2 * CHUNK)
    ti = pl.program_id(1)

    @pl.when(ti == 0)
    def _():
        zh = jnp.zeros((HEAD_N, HEAD_N), F32)
        for i in range(nb):
            for p in range(n_pairs):
                sbd_ref[i, p] = jnp.concatenate(
                    [jnp.concatenate([s0_ref[i, 2 * p], zh], axis=1),
                     jnp.concatenate([zh, s0_ref[i, 2 * p + 1]], axis=1)], axis=0)

    for i in range(nb):
        r0 = top + i * tt
        sh_ref[r0:r0 + tt, 0:d] = r_ref[i]
        sh_ref[r0:r0 + tt, d:2 * d] = k_ref[i]
        sh_ref[r0:r0 + tt, 2 * d:3 * d] = v_ref[i]
        sh_ref[r0:r0 + tt, 3 * d:3 * d + n_lora] = lo_ref[i]

    @pl.when(ti == 0)
    def _():
        for i in range(nb):
            sh_ref[top - 1 + i * tt:top + i * tt, :] = prev_ref[i]

    li = lax.broadcasted_iota(jnp.int32, (LANES, LANES), 0) // HEAD_N
    lj = lax.broadcasted_iota(jnp.int32, (LANES, LANES), 1) // HEAD_N
    same_head = li == lj
    wi = lax.broadcasted_iota(jnp.int32, (MXU_N, MXU_N), 0) // HEAD_N
    wj = lax.broadcasted_iota(jnp.int32, (MXU_N, MXU_N), 1) // HEAD_N
    ones_bd = jnp.where(wi == wj, 1.0, 0.0).astype(BF16)

    def head_sum(x):
        parts = [_dot(x[:, c0:c0 + MXU_N].astype(BF16), ones_bd) for c0 in range(0, d, MXU_N)]
        return jnp.concatenate(parts, axis=1)

    ri = lax.broadcasted_iota(jnp.int32, (sub, sub), 0)
    rj = lax.broadcasted_iota(jnp.int32, (sub, sub), 1)
    same_seg = (ri // seg) == (rj // seg)
    tril_blk = jnp.where(rj <= ri, jnp.where(same_seg, 1.0, 0.0), 0.0).astype(BF16)

    def prep(q0):
        def mixed(c0, c1):
            cur = sh_ref[top + q0:top + q0 + sub, c0:c1]
            prv = sh_ref[top - 1 + q0:top - 1 + q0 + sub, c0:c1]
            return cur + (prv - cur) * mu_ref[:, c0:c1]

        lora = mixed(3 * d, 3 * d + n_lora)
        l_di = lora[:, 0:LANES]
        l_g = lora[:, LANES:2 * LANES]
        dec_in = dbase_ref[...] + _dot(jnp.tanh(l_di).astype(BF16), wdec_ref[...])
        a = _sigmoid(ibase_ref[...] + _dot(l_di.astype(BF16), wiclr_ref[...]))
        g_s[q0:q0 + sub, :] = _dot(_sigmoid(l_g).astype(BF16), wgate_ref[...])
        yield
        z = -dec_in
        softplus = jnp.maximum(z, 0.0) + jnp.log(1.0 + jnp.exp(-jnp.abs(z)))
        lw = -jnp.exp(-softplus - 0.5)
        live = None
        if t_valid < tt:
            live = (lax.broadcasted_iota(jnp.int32, (sub, d), 0) + q0) % tt < t_valid
            lw = jnp.where(live, lw, 0.0)
        hi, mid, lo = _split3(lw)
        cum = _dot(tril_blk, hi) + _dot(tril_blk, mid) + _dot(tril_blk, lo)
        tot = jnp.concatenate([jnp.broadcast_to(cum[e - 1:e, :], (seg, d)) for e in range(seg, sub + 1, seg)],
                              axis=0)
        dec_all = jnp.exp(tot)
        for sg in range(sub // seg):
            dall_s[q0 // seg + sg:q0 // seg + sg + 1, :] = dec_all[sg * seg:sg * seg + 1, :]
        yield

        def put(ref, val, mask=True):
            if mask and live is not None:
                val = jnp.where(live, val, 0.0)
            ref[q0:q0 + sub, :] = val.astype(ref.dtype)

        k = mixed(d, 2 * d)
        kk = k * kkw_ref[...]
        kk = kk * lax.rsqrt(jnp.maximum(head_sum(kk * kk), KK_NORM_FLOOR))
        yield
        put(at_b, -kk * jnp.exp(cum - lw))
        beta = kk * a
        put(bh_b, beta * jnp.exp(-cum))
        yield
        put(be_b, beta * jnp.exp(tot - cum))
        kt = k * (1.0 + (a - 1.0) * ka_ref[...])
        yield
        put(kh_b, kt * jnp.exp(-cum))
        put(ke_b, kt * jnp.exp(tot - cum))
        yield
        r = mixed(0, d)
        v = mixed(2 * d, 3 * d)
        bo_s[q0:q0 + sub, :] = head_sum(r * kt * rk_ref[...]) * v
        yield
        put(rt_b, r * jnp.exp(cum), mask=False)
        put(v_b, v)
        yield

    pt = lax.broadcasted_iota(jnp.int32, (CHUNK, LANES), 0)
    pl_ = lax.broadcasted_iota(jnp.int32, (CHUNK, LANES), 1)
    ps = pl_ % HEAD_N
    strict = ps < pt
    incl = ps <= pt
    if seg < CHUNK:
        own = (ps // seg) == (pt // seg)
        strict = jnp.logical_and(own, strict)
        incl = jnp.logical_and(own, incl)
    lane_lo = pl_ < HEAD_N
    eye_pair = jnp.where(ps == pt, 1.0, 0.0)
    zc = jnp.zeros((CHUNK, LANES), F32)

    in_lo = jnp.where(lane_lo, 1.0, 0.0).astype(BF16) > 0
    zb = jnp.zeros((CHUNK, LANES), BF16)

    def bd(xb):
        return jnp.concatenate([jnp.where(in_lo, xb, zb), jnp.where(in_lo, zb, xb)], axis=0)

    def blk(ref, c, p):
        return ref[c * CHUNK:(c + 1) * CHUNK, p * LANES:(p + 1) * LANES]

    def blkb(ref, c, p):
        return blk(ref, c, p).astype(BF16)

    def put_blk(ref, c, p, val):
        ref[c * CHUNK:(c + 1) * CHUNK, p * LANES:(p + 1) * LANES] = val

    def chunk_setup(its):
        ar = [_dot_nt(jnp.concatenate([blkb(at_b, c, p), blkb(rt_b, c, p)], axis=0),
                      jnp.concatenate([bd(blkb(bh_b, c, p)), bd(blkb(kh_b, c, p))], axis=0))
              for c, p in its]
        yield
        l_ab = [jnp.where(strict, a[0:CHUNK, 0:LANES], zc) for a in ar]
        for (c, p), a in zip(its, ar):
            put_blk(arb_s, c, p, jnp.where(incl, a[CHUNK:2 * CHUNK, 0:LANES], zc).astype(BF16))
        lv_yv = [_dot(jnp.concatenate([jnp.where(strict, a[0:CHUNK, LANES:2 * LANES], zc),
                                       jnp.where(incl, a[CHUNK:2 * CHUNK, LANES:2 * LANES], zc)],
                                      axis=0).astype(BF16), bd(blkb(v_b, c, p)))
                 for (c, p), a in zip(its, ar)]
        yield
        for (c, p), m in zip(its, lv_yv):
            put_blk(lv_s, c, p, m[0:CHUNK])
            put_blk(yv_s, c, p, m[CHUNK:2 * CHUNK])
        tm = [eye_pair + l for l in l_ab]
        if n_sq > 0:
            pw = [_dot(l.astype(BF16), bd(l.astype(BF16))) for l in l_ab]
            yield
        for i in range(n_sq):
            if i == n_sq - 1:
                tm = [t + _dot(q.astype(BF16), bd(t.astype(BF16))) for t, q in zip(tm, pw)]
            else:
                pr = [_dot(q.astype(BF16), jnp.concatenate([bd(q.astype(BF16)), bd(t.astype(BF16))], axis=1))
                      for t, q in zip(tm, pw)]
                pw = [x[:, 0:LANES] for x in pr]
                tm = [t + x[:, LANES:2 * LANES] for t, x in zip(tm, pr)]
            yield
        for (c, p), t in zip(its, tm):
            put_blk(tm_s, c, p, t.astype(BF16))
        yield

    pairs = range(n_pairs)
    segs = range(CHUNK // seg)

    def cat_rows(parts):
        return parts[0] if len(parts) == 1 else jnp.concatenate(parts, axis=0)

    def stack_bf16(upper, lower):
        if seg % 16 == 0:
            return jnp.concatenate([upper.astype(BF16), lower.astype(BF16)], axis=0)
        return jnp.concatenate([upper.astype(F32), lower.astype(F32)], axis=0).astype(BF16)

    def chunk_step(c):
        def seq_of(sg):
            return (c * CHUNK + sg * seg) // tt if nb > 1 else 0

        def seg_rows(ref, sg, p):
            r0 = c * CHUNK + sg * seg
            return ref[r0:r0 + seg, p * LANES:(p + 1) * LANES]

        s_old = [[sbd_ref[seq_of(sg), p] for p in pairs] for sg in segs]
        x = [[_dot_nt(stack_bf16(seg_rows(at_b, sg, p), seg_rows(rt_b, sg, p)), s_old[sg][p].astype(BF16))
              for p in pairs] for sg in segs]
        yield
        x_a = [cat_rows([x[sg][p][0:seg] for sg in segs]) for p in pairs]
        x_r = [cat_rows([x[sg][p][seg:2 * seg] for sg in segs]) for p in pairs]
        u = [_dot(blk(tm_s, c, p), bd((x_a[p] + blk(lv_s, c, p)).astype(BF16))) for p in pairs]
        yield
        ub = [t.astype(BF16) for t in u]
        for p in pairs:
            put_blk(y_s, c, p, x_r[p] + blk(yv_s, c, p) + _dot(blk(arb_s, c, p), bd(ub[p])))
        yield
        for sg in segs:
            upd = [_dot_tn(stack_bf16(u[p][sg * seg:(sg + 1) * seg], seg_rows(v_b, sg, p)),
                           stack_bf16(seg_rows(be_b, sg, p), seg_rows(ke_b, sg, p))) for p in pairs]
            gi = c * (CHUNK // seg) + sg
            for p in pairs:
                sbd_ref[seq_of(sg), p] = (s_old[sg][p] * dall_s[gi:gi + 1, p * LANES:(p + 1) * LANES]
                                          + jnp.where(same_head, upd[p], 0.0))
            yield

    def finish(q0):
        y_n = []
        for c0 in range(0, d, MXU_N):
            yb = y_s[q0:q0 + sub, c0:c0 + MXU_N]
            yc = yb - _dot(yb.astype(BF16), ones_bd) * (1.0 / HEAD_N)
            var = _dot((yc * yc).astype(BF16), ones_bd) * (1.0 / HEAD_N)
            y_n.append(yc * lax.rsqrt(var + GN_EPS))
            yield
        y_n = jnp.concatenate(y_n, axis=1) * lng_ref[...] + lnb_ref[...]
        out = _dot(((y_n + bo_s[q0:q0 + sub, :]) * g_s[q0:q0 + sub, :]).astype(BF16), wout_ref[...])
        if nb == 1:
            ob_ref[0, q0:q0 + sub, :] = out
        else:
            for i in range(sub // tt):
                ob_ref[q0 // tt + i] = out[i * tt:(i + 1) * tt]
        yield

    cps = sub // CHUNK
    n_sub = rows_t // sub

    def chunk_walk(s):
        for c in range(cps):
            yield from chunk_step(s * cps + c)

    def round_robin(gens):
        while gens:
            for gen in list(gens):
                if next(gen, StopIteration) is StopIteration:
                    gens.remove(gen)

    for slot in range(n_sub + 3):
        live_gens = []
        if slot < n_sub:
            live_gens.append(prep(slot * sub))
        if 0 <= slot - 1 < n_sub:
            live_gens.append(chunk_setup([((slot - 1) * cps + c, p) for c in range(cps) for p in pairs]))
        if 0 <= slot - 2 < n_sub:
            live_gens.append(chunk_walk(slot - 2))
        if 0 <= slot - 3 < n_sub:
            live_gens.append(finish((slot - 3) * sub))
        round_robin(live_gens)
        if slot == 0 and nb == 1:
            sh_ref[top - 1:top, :] = sh_ref[top - 1 + t_valid:top + t_valid, :]

    @pl.when(ti == pl.num_programs(1) - 1)
    def _():
        for i in range(nb):
            for p in range(n_pairs):
                sout_ref[i, 2 * p] = sbd_ref[i, p, 0:HEAD_N, 0:HEAD_N]
                sout_ref[i, 2 * p + 1] = sbd_ref[i, p, HEAD_N:2 * HEAD_N, HEAD_N:2 * HEAD_N]


def _rwkv_branch(proj, prev0, s0, mu, dbase, wdec, ibase, wiclr, wgate, kkw, ka, rk, lng, lnb, wout,
                 *, nb, tt, t_valid):
    b, t, _ = proj.shape
    d = wout.shape[0]
    n_lora = wgate.shape[0] * 2
    n_pairs = d // LANES
    rows_s = nb * tt
    seg = min(tt, CHUNK)
    assert b % nb == 0 and t % tt == 0 and rows_s % CHUNK == 0 and max(tt, CHUNK) % seg == 0
    assert nb == 1 or (t == tt and t_valid < tt)
    n_sq = max(0, math.ceil(math.log2(min(seg, t_valid))) - 1)
    kern = functools.partial(_rwkv_kernel, nb=nb, tt=tt, t_valid=t_valid, n_sq=n_sq)
    const = lambda shape: pl.BlockSpec(shape, lambda i, j: (0,) * len(shape))
    col = lambda c: pl.BlockSpec((nb, tt, d), lambda i, j: (i, j, c))
    big = lambda dt: pltpu.VMEM((rows_s, d), dt)
    op_dt = BF16 if seg % 16 == 0 else F32
    return pl.pallas_call(
        kern,
        grid=(b // nb, t // tt),
        in_specs=[col(4), col(5), col(6),
                  pl.BlockSpec((nb, tt, n_lora), lambda i, j: (i, j, 7 * d // n_lora)),
                  pl.BlockSpec((nb, 1, 3 * d + n_lora), lambda i, j: (i, 0, 0)),
                  pl.BlockSpec((nb, 2 * n_pairs, HEAD_N, HEAD_N), lambda i, j: (i, 0, 0, 0)),
                  const((1, 3 * d + n_lora)), const((1, d)), const((LANES, d)), const((1, d)),
                  const((LANES, d)), const((LANES, d)), const((1, d)), const((1, d)), const((1, d)),
                  const((1, d)), const((1, d)), const((d, d))],
        out_specs=[pl.BlockSpec((nb, tt, d), lambda i, j: (i, j, 0)),
                   pl.BlockSpec((nb, 2 * n_pairs, HEAD_N, HEAD_N), lambda i, j: (i, 0, 0, 0))],
        out_shape=[jax.ShapeDtypeStruct((b, t, d), F32),
                   jax.ShapeDtypeStruct((b, 2 * n_pairs, HEAD_N, HEAD_N), F32)],
        scratch_shapes=[pltpu.VMEM((SUBLANES + rows_s, 3 * d + n_lora), F32),
                        big(op_dt), big(op_dt), big(BF16), big(BF16), big(op_dt), big(op_dt), big(op_dt),
                        pltpu.VMEM((max(rows_s // seg, SUBLANES), d), F32),
                        big(F32), big(F32),
                        big(BF16), big(BF16), big(F32), big(F32), big(F32),
                        pltpu.VMEM((nb, n_pairs, LANES, LANES), F32)],
        compiler_params=_cparams(2),
        name="rwkv_branch",
    )(proj, proj, proj, proj, prev0, s0, mu, dbase, wdec, ibase, wiclr, wgate, kkw, ka, rk, lng, lnb, wout)


def _ffn_kernel(x_ref, ga_ref, gb_ref, oa_ref, ob_ref, wo_ref, g1_ref, g2_ref, wu_ref, wd_ref, g3_ref,
                o_ref, *, ff_chunk):
    merged = ga_ref[...] * oa_ref[...] + gb_ref[...] * ob_ref[...]
    m = _dot(merged.astype(BF16), wo_ref[...])
    x1 = x_ref[...] + _rms(m, g1_ref[...])
    h2 = _rms(x1, g2_ref[...]).astype(BF16)
    f = None
    for c0 in range(0, wu_ref.shape[1], ff_chunk):
        up = jnp.maximum(_dot(h2, wu_ref[:, c0:c0 + ff_chunk]), 0.0)
        part = _dot((up * up).astype(BF16), wd_ref[c0:c0 + ff_chunk, :])
        f = part if f is None else f + part
    o_ref[...] = x1 + _rms(f, g3_ref[...])


def _merge_ffn(x, proj, o_a, o_b, w_out, g1, g2, w_up, w_down, g3, *, tm):
    m, d = x.shape
    dff = w_up.shape[1]
    tok = lambda c: pl.BlockSpec((tm, d), lambda i: (i, c))
    const = lambda shape: pl.BlockSpec(shape, lambda i: (0, 0), pipeline_mode=pl.Buffered(1))
    return pl.pallas_call(
        functools.partial(_ffn_kernel, ff_chunk=d),
        grid=(m // tm,),
        in_specs=[tok(0), tok(2), tok(3), tok(0), tok(0),
                  const((d, d)), const((1, d)), const((1, d)), const((d, dff)), const((dff, d)),
                  const((1, d))],
        out_specs=tok(0),
        out_shape=jax.ShapeDtypeStruct((m, d), F32),
        compiler_params=_cparams(1),
        name="merge_ffn",
    )(x, proj, proj, o_a, o_b, w_out, g1, g2, w_up, w_down, g3)


def _layer(x, conv_state, prev0, s0, w, *, t_valid, tt, nb):
    b, t, d = x.shape
    x2 = x.reshape(b * t, d)
    tm = min(TOKEN_TILE, b * t)
    proj = _norm_matmul(x2, w["pre_mix_g"], w["w_in"], tm=tm, tn=w["w_in"].shape[1] // 2,
                        apply_norm=True, name="in_proj", glu_width=d)
    proj3 = proj.reshape(b, t, -1)
    o_b, s_new = _rwkv_branch(proj3, prev0, s0, w["mu"], w["decay_base"], w["w_dec"], w["iclr_base"],
                              w["w_iclr"], w["w_gate"], w["k_k"], w["k_a"], w["r_k"], w["lnx_g"],
                              w["lnx_b"], w["w_rwkv_out"], nb=nb, tt=tt, t_valid=min(t_valid, tt))
    o_a, tail = _conv_branch(proj3, conv_state, w["conv_w"], w["conv_b"], w["conv_ln_g"], w["conv_ln_b"],
                             w["w_conv_out"], nb=nb, tt=tt, t_valid=min(t_valid, tt))
    y = _merge_ffn(x2, proj, o_a.reshape(b * t, d), o_b.reshape(b * t, d), w["w_out"], w["post_mix_g"],
                   w["pre_ffn_g"], w["w_ff_up"], w["w_ff_down"], w["post_ffn_g"], tm=tm)
    return y.reshape(b, t, d), tail, s_new


def kernel(x_prompt, x_sample, state_conv, state_shift, state_wkv, pre_mix_g, post_mix_g, pre_ffn_g, post_ffn_g, w_in, conv_w, conv_b, conv_ln_g, conv_ln_b, w_conv_out, shift_mu, decay_base, w_decay_up, iclr_base, w_iclr_up, w_gate_up, k_k, k_a, r_k, lnx_g, lnx_b, w_rwkv_out, w_out, w_ff_up, w_ff_down):
    depth = w_in.shape[0]
    assert depth == 1
    d = x_prompt.shape[-1]
    n_heads = state_wkv.shape[2]
    r_dec, r_iclr, r_gate = w_decay_up.shape[1], w_iclr_up.shape[1], w_gate_up.shape[1]
    assert n_heads * HEAD_N == d and r_dec + r_iclr == LANES and r_gate == LANES
    n_lora = r_dec + r_iclr + r_gate
    i0 = 2 * d
    i1 = i0 + 3 * d + n_lora
    row = lambda a: a.reshape(1, -1)

    w_in_l = w_in[0]
    w_in_p = jnp.concatenate([w_in_l[:, :i0], w_in_l[:, i1:], w_in_l[:, i0:i1]], axis=1).astype(BF16)
    zpad = lambda n: jnp.zeros((n, d), F32)
    w = dict(
        pre_mix_g=row(pre_mix_g[0]), post_mix_g=row(post_mix_g[0]), pre_ffn_g=row(pre_ffn_g[0]),
        post_ffn_g=row(post_ffn_g[0]), w_in=w_in_p,
        conv_w=conv_w[0], conv_b=row(conv_b[0]), conv_ln_g=row(conv_ln_g[0]), conv_ln_b=row(conv_ln_b[0]),
        w_conv_out=w_conv_out[0].astype(BF16), mu=row(shift_mu[0]),
        decay_base=row(decay_base[0]), iclr_base=row(iclr_base[0]),
        w_dec=jnp.concatenate([w_decay_up[0], zpad(r_iclr)], axis=0).astype(BF16),
        w_iclr=jnp.concatenate([zpad(r_dec), w_iclr_up[0]], axis=0).astype(BF16),
        w_gate=w_gate_up[0].astype(BF16),
        k_k=row(k_k[0]), k_a=row(k_a[0]), r_k=row(r_k[0]), lnx_g=row(lnx_g[0]), lnx_b=row(lnx_b[0]),
        w_rwkv_out=w_rwkv_out[0].astype(BF16), w_out=w_out[0].astype(BF16),
        w_ff_up=w_ff_up[0].astype(BF16), w_ff_down=w_ff_down[0].astype(BF16),
    )

    bp, tp, _ = x_prompt.shape
    yp, conv_p, s_p = _layer(
        x_prompt, jnp.zeros((bp, conv_w.shape[1] - 1, d), F32), jnp.zeros((bp, 1, 3 * d + n_lora), F32),
        jnp.zeros((bp, n_heads, HEAD_N, HEAD_N), F32), w, t_valid=tp, tt=TOKEN_TILE, nb=1)
    shift_p = _rms_rows(x_prompt[:, -1], w["pre_mix_g"])

    bs, ts, _ = x_sample.shape
    t_pad = SUBLANES
    assert ts < t_pad
    xs = jnp.pad(x_sample, ((0, 0), (0, t_pad - ts), (0, 0)))
    prev0 = _norm_matmul(state_shift[0], w["pre_mix_g"], w_in_p[:, 4 * d:], tm=bs, tn=3 * d + n_lora,
                         apply_norm=False, name="shift_proj")
    ys, conv_s, s_s = _layer(
        xs, state_conv.reshape(state_conv.shape[1:]), prev0.reshape(bs, 1, -1),
        state_wkv.reshape(state_wkv.shape[1:]), w,
        t_valid=ts, tt=t_pad, nb=CHUNK // t_pad)
    shift_s = _rms_rows(x_sample[:, -1], w["pre_mix_g"])

    return (yp, ys[:, :ts], conv_p[None], shift_p[None], s_p[None],
            conv_s[None], shift_s[None], s_s[None])
```
